```python
import math
import jax, jax.numpy as jnp
from jax import lax
import numpy as np

D_MODEL = 2048
BATCH = 16
SEQ = 2048
DEPTH = 2
DEC_BATCH = 32
DEC_SEQ = 32
PAST_LEN = 2048

CHUNK = 64
CONV_WIDTH = D_MODEL // 4
CONV_K = 31
CONV_BUF = CONV_K - 1
SSM_WIDTH = D_MODEL // 4
SSM_GROUP = 16
SSM_GROUPS = SSM_WIDTH // SSM_GROUP
SSM_STATE = 64
ATTN_HEADS = 16
HEAD_DIM = 64
ATTN_WIDTH = ATTN_HEADS * HEAD_DIM
Q_BLOCK = 128
ATTN_SCALE = HEAD_DIM ** -0.5
N_BRANCH = 3
N_EXPERT_GROUPS = 4
EXPERTS_PER_GROUP = 8
N_EXPERTS = N_EXPERT_GROUPS * EXPERTS_PER_GROUP
TOP_K_IN_GROUP = 2
D_EXPERT = D_MODEL // 8
EPS = 1e-6
IN_SPLITS = (2 * CONV_WIDTH, SSM_WIDTH, ATTN_WIDTH, ATTN_WIDTH, ATTN_WIDTH, ATTN_HEADS, N_BRANCH * D_MODEL)
D_IN = sum(IN_SPLITS)
FORGET_OFFSET = 2 * CONV_WIDTH + SSM_WIDTH + 3 * ATTN_WIDTH

kernel_name = "hybrid_streaming_encoder_step"


def rms_norm(x, g):
    xf = x.astype(jnp.float32)
    y = xf * lax.rsqrt(jnp.mean(xf * xf, axis=-1, keepdims=True) + EPS)
    return (y * g.astype(jnp.float32)).astype(x.dtype)


def layer_norm(x, g, b):
    xf = x.astype(jnp.float32)
    mu = jnp.mean(xf, axis=-1, keepdims=True)
    var = jnp.mean(jnp.square(xf - mu), axis=-1, keepdims=True)
    y = (xf - mu) * lax.rsqrt(var + EPS)
    return (y * g.astype(jnp.float32) + b.astype(jnp.float32)).astype(x.dtype)


def split_columns(z):
    idx, off = [], 0
    for w in IN_SPLITS[:-1]:
        off += w
        idx.append(off)
    return jnp.split(z, idx, axis=-1)


def conformer_conv(z_conv, conv_buf, w_dw, b_dw, ln_g, ln_b, w_conv_out):
    a, b = jnp.split(z_conv, 2, axis=-1)
    u = a * jax.nn.sigmoid(b)
    cin = jnp.concatenate([conv_buf.astype(u.dtype), u], axis=1)
    h = lax.conv_general_dilated(cin, w_dw[:, None, :].astype(u.dtype), (1,), 'VALID',
                                 dimension_numbers=('NWC', 'WIO', 'NWC'),
                                 feature_group_count=CONV_WIDTH) + b_dw
    h = jax.nn.silu(layer_norm(h, ln_g, ln_b))
    return h @ w_conv_out, cin[:, -CONV_BUF:]


def s5_scan(u, h0, a_re, a_im, log_dt, b_re, b_im, c_re, c_im, d_skip):
    f32 = jnp.float32
    bsz, L, _ = u.shape
    uf = u.astype(f32).reshape(bsz, L, SSM_GROUPS, SSM_GROUP)
    lam = lax.complex(jnp.minimum(a_re.astype(f32), -1e-4), a_im.astype(f32))
    dt = jnp.exp(log_dt.astype(f32))[:, None]
    lam_bar = jnp.exp(lam * dt)
    b_bar = ((lam_bar - 1.0) / lam)[..., None] * lax.complex(b_re.astype(f32), b_im.astype(f32))
    bu = jnp.einsum('blgc,gpc->blgp', uf.astype(jnp.complex64), b_bar)
    a_seq = jnp.broadcast_to(lam_bar, (1, L) + lam_bar.shape)

    def combine(e1, e2):
        a1, x1 = e1
        a2, x2 = e2
        return a1 * a2, a2 * x1 + x2

    a_cum, h = lax.associative_scan(combine, (a_seq, bu), axis=1)
    if h0 is not None:
        h = h + a_cum * lax.complex(h0[..., 0].astype(f32), h0[..., 1].astype(f32))[:, None]
    c = lax.complex(c_re.astype(f32), c_im.astype(f32))
    y = jnp.real(jnp.einsum('blgp,gcp->blgc', h, c)) + d_skip.astype(f32).reshape(SSM_GROUPS, SSM_GROUP) * uf
    h_last = h[:, -1]
    return y.reshape(bsz, L, SSM_WIDTH), jnp.stack([jnp.real(h_last), jnp.imag(h_last)], axis=-1)


def fox_prompt(q, k, v, logf):
    bsz, L = q.shape[0], q.shape[1]
    nb = L // Q_BLOCK
    c = jnp.cumsum(logf, axis=1)
    cT = c.transpose(0, 2, 1)
    qb = q.reshape(bsz, nb, Q_BLOCK, ATTN_HEADS, HEAD_DIM).swapaxes(0, 1)
    cb = c.reshape(bsz, nb, Q_BLOCK, ATTN_HEADS).swapaxes(0, 1)
    pos_k = jnp.arange(L)

    def one_block(args):
        qi, ci, i = args
        s = jnp.einsum('bqhd,bkhd->bhqk', qi, k).astype(jnp.float32) * ATTN_SCALE
        s = s + ci.transpose(0, 2, 1)[..., None] - cT[:, :, None, :]
        pos_q = i * Q_BLOCK + jnp.arange(Q_BLOCK)
        s = jnp.where(pos_k[None, :] <= pos_q[:, None], s, -jnp.inf)
        p = jax.nn.softmax(s, axis=-1).astype(v.dtype)
        return jnp.einsum('bhqk,bkhd->bqhd', p, v)

    o = lax.map(one_block, (qb, cb, jnp.arange(nb)))
    return o.swapaxes(0, 1).reshape(bsz, L, ATTN_WIDTH)


def fox_sample(q, k_new, v_new, logf_new, k_past, v_past, logf_past):
    bsz, n_new = q.shape[0], q.shape[1]
    n_past = k_past.shape[1]
    k = jnp.concatenate([k_past.astype(k_new.dtype), k_new], axis=1)
    v = jnp.concatenate([v_past.astype(v_new.dtype), v_new], axis=1)
    c = jnp.cumsum(jnp.concatenate([logf_past.astype(jnp.float32), logf_new], axis=1), axis=1)
    cT = c.transpose(0, 2, 1)
    s = jnp.einsum('bqhd,bkhd->bhqk', q, k).astype(jnp.float32) * ATTN_SCALE
    s = s + cT[:, :, n_past:, None] - cT[:, :, None, :]
    allowed = jnp.arange(n_past + n_new)[None, :] <= (n_past + jnp.arange(n_new))[:, None]
    s = jnp.where(allowed, s, -jnp.inf)
    p = jax.nn.softmax(s, axis=-1).astype(v.dtype)
    return jnp.einsum('bhqk,bkhd->bqhd', p, v).reshape(bsz, n_new, ATTN_WIDTH)


def mixer_sublayer(x, p, conv_buf, ssm_h0, past):
    bsz, L, _ = x.shape
    xn = rms_norm(x, p['g_mix'])
    z = xn @ p['w_in'] + p['b_in']
    z_conv, z_u, z_q, z_k, z_v, z_f, z_g = split_columns(z)
    if conv_buf is None:
        conv_buf = jnp.zeros((bsz, CONV_BUF, CONV_WIDTH), x.dtype)
    out_a, new_conv = conformer_conv(z_conv, conv_buf, p['w_dw'], p['b_dw'], p['ln_g'], p['ln_b'], p['w_conv_out'])
    y_s, new_ssm = s5_scan(z_u, ssm_h0, p['ssm_a_re'], p['ssm_a_im'], p['ssm_log_dt'], p['ssm_b_re'],
                           p['ssm_b_im'], p['ssm_c_re'], p['ssm_c_im'], p['ssm_d'])
    hs = jax.nn.gelu(y_s).astype(x.dtype)
    hs = hs * jax.nn.sigmoid(hs @ p['w_glu'] + p['b_glu'])
    out_b = hs @ p['w_ssm_out']
    q = rms_norm(z_q.reshape(bsz, L, ATTN_HEADS, HEAD_DIM), p['q_gain'])
    k = rms_norm(z_k.reshape(bsz, L, ATTN_HEADS, HEAD_DIM), p['k_gain'])
    v = z_v.reshape(bsz, L, ATTN_HEADS, HEAD_DIM)
    logf = jax.nn.log_sigmoid(z_f.astype(jnp.float32))
    if past is None:
        o = fox_prompt(q, k, v, logf)
    else:
        o = fox_sample(q, k, v, logf, past[0], past[1], past[2])
    out_c = o @ p['w_attn_out']
    g_a, g_b, g_c = jnp.split(jax.nn.sigmoid(z_g), N_BRANCH, axis=-1)
    y = (g_a * out_a + g_b * out_b + g_c * out_c) @ p['w_out']
    return x + y, (new_conv, new_ssm, k, v, logf)


def moe_sublayer(x, p):
    f32 = jnp.float32
    xn = rms_norm(x, p['g_ffn'])
    pg = jax.nn.softmax((xn @ p['w_route_g'] + p['b_route_g']).astype(f32), axis=-1)
    g_top = jnp.argmax(pg, axis=-1)
    p_top = jnp.max(pg, axis=-1)
    le = (xn @ p['w_route_e'] + p['b_route_e']).astype(f32)
    le = le.reshape(le.shape[:-1] + (N_EXPERT_GROUPS, EXPERTS_PER_GROUP))
    le_g = jnp.einsum('blge,blg->ble', le, jax.nn.one_hot(g_top, N_EXPERT_GROUPS, dtype=f32))
    top_v, top_i = lax.top_k(le_g, TOP_K_IN_GROUP)
    w_sel = jax.nn.softmax(top_v, axis=-1) * p_top[..., None]
    ids = g_top[..., None] * EXPERTS_PER_GROUP + top_i
    comb = jnp.sum(jax.nn.one_hot(ids, N_EXPERTS, dtype=f32) * w_sel[..., None], axis=-2).astype(x.dtype)
    out = jnp.zeros_like(x)
    for e in range(N_EXPERTS):
        h = jax.nn.silu(xn @ p['w_e_gate'][e]) * (xn @ p['w_e_up'][e])
        out = out + comb[..., e:e + 1] * (h @ p['w_e_down'][e])
    return x + out


def setup_inputs(seed: int = 0) -> dict:
    key = jax.random.key(seed)
    ks = iter(jax.random.split(key, 48))
    f32 = jnp.float32

    def nrm(shape, scale):
        return scale * jax.random.normal(next(ks), shape, f32)

    b_in = nrm((DEPTH, D_IN), 0.02)
    b_in = b_in.at[:, FORGET_OFFSET:FORGET_OFFSET + ATTN_HEADS].add(jnp.linspace(1.0, 4.0, ATTN_HEADS))
    a_im = jnp.broadcast_to(math.pi * jnp.arange(SSM_STATE, dtype=f32), (DEPTH, SSM_GROUPS, SSM_STATE))
    return {
        'x_prompt': nrm((BATCH, SEQ, D_MODEL), 1.0),
        'x_sample': nrm((DEC_BATCH, DEC_SEQ, D_MODEL), 1.0),
        'cache_conv': nrm((DEPTH, DEC_BATCH, CONV_BUF, CONV_WIDTH), 0.5),
        'state_ssm': nrm((DEPTH, DEC_BATCH, SSM_GROUPS, SSM_STATE, 2), 0.1),
        'cache_k': nrm((DEPTH, DEC_BATCH, PAST_LEN, ATTN_HEADS, HEAD_DIM), 1.0),
        'cache_v': nrm((DEPTH, DEC_BATCH, PAST_LEN, ATTN_HEADS, HEAD_DIM), 1.0),
        'cache_logf': jax.nn.log_sigmoid(2.5 + nrm((DEPTH, DEC_BATCH, PAST_LEN, ATTN_HEADS), 1.0)),
        'g_mix': 1.0 + nrm((DEPTH, D_MODEL), 0.02),
        'w_in': nrm((DEPTH, D_MODEL, D_IN), D_MODEL ** -0.5),
        'b_in': b_in,
        'w_dw': nrm((DEPTH, CONV_K, CONV_WIDTH), CONV_K ** -0.5),
        'b_dw': nrm((DEPTH, CONV_WIDTH), 0.02),
        'ln_g': 1.0 + nrm((DEPTH, CONV_WIDTH), 0.02),
        'ln_b': nrm((DEPTH, CONV_WIDTH), 0.02),
        'w_conv_out': nrm((DEPTH, CONV_WIDTH, D_MODEL), CONV_WIDTH ** -0.5),
        'ssm_a_re': -0.5 + nrm((DEPTH, SSM_GROUPS, SSM_STATE), 0.01),
        'ssm_a_im': a_im + nrm((DEPTH, SSM_GROUPS, SSM_STATE), 0.01),
        'ssm_log_dt': jax.random.uniform(next(ks), (DEPTH, SSM_GROUPS), f32, math.log(1e-3), math.log(1e-1)),
        'ssm_b_re': nrm((DEPTH, SSM_GROUPS, SSM_STATE, SSM_GROUP), (2 * SSM_GROUP) ** -0.5),
        'ssm_b_im': nrm((DEPTH, SSM_GROUPS, SSM_STATE, SSM_GROUP), (2 * SSM_GROUP) ** -0.5),
        'ssm_c_re': nrm((DEPTH, SSM_GROUPS, SSM_GROUP, SSM_STATE), (2 * SSM_STATE) ** -0.5),
        'ssm_c_im': nrm((DEPTH, SSM_GROUPS, SSM_GROUP, SSM_STATE), (2 * SSM_STATE) ** -0.5),
        'ssm_d': nrm((DEPTH, SSM_WIDTH), 1.0),
        'w_glu': nrm((DEPTH, SSM_WIDTH, SSM_WIDTH), SSM_WIDTH ** -0.5),
        'b_glu': nrm((DEPTH, SSM_WIDTH), 0.02),
        'w_ssm_out': nrm((DEPTH, SSM_WIDTH, D_MODEL), SSM_WIDTH ** -0.5),
        'q_gain': 1.0 + nrm((DEPTH, HEAD_DIM), 0.02),
        'k_gain': 1.0 + nrm((DEPTH, HEAD_DIM), 0.02),
        'w_attn_out': nrm((DEPTH, ATTN_WIDTH, D_MODEL), ATTN_WIDTH ** -0.5),
        'w_out': nrm((DEPTH, D_MODEL, D_MODEL), D_MODEL ** -0.5),
        'g_ffn': 1.0 + nrm((DEPTH, D_MODEL), 0.02),
        'w_route_g': nrm((DEPTH, D_MODEL, N_EXPERT_GROUPS), D_MODEL ** -0.5),
        'b_route_g': nrm((DEPTH, N_EXPERT_GROUPS), 0.01),
        'w_route_e': nrm((DEPTH, D_MODEL, N_EXPERTS), D_MODEL ** -0.5),
        'b_route_e': nrm((DEPTH, N_EXPERTS), 0.01),
        'w_e_gate': nrm((DEPTH, N_EXPERTS, D_MODEL, D_EXPERT), D_MODEL ** -0.5),
        'w_e_up': nrm((DEPTH, N_EXPERTS, D_MODEL, D_EXPERT), D_MODEL ** -0.5),
        'w_e_down': nrm((DEPTH, N_EXPERTS, D_EXPERT, D_MODEL), D_EXPERT ** -0.5),
    }


def reference(x_prompt, x_sample, cache_conv, state_ssm, cache_k, cache_v, cache_logf,
              g_mix, w_in, b_in, w_dw, b_dw, ln_g, ln_b, w_conv_out,
              ssm_a_re, ssm_a_im, ssm_log_dt, ssm_b_re, ssm_b_im, ssm_c_re, ssm_c_im, ssm_d,
              w_glu, b_glu, w_ssm_out, q_gain, k_gain, w_attn_out, w_out,
              g_ffn, w_route_g, b_route_g, w_route_e, b_route_e, w_e_gate, w_e_up, w_e_down):
    xp, xs = x_prompt, x_sample
    conv_p, ssm_p, k_p, v_p, lf_p = [], [], [], [], []
    conv_s, ssm_s, k_s, v_s, lf_s = [], [], [], [], []
    for l in range(DEPTH):
        p = dict(g_mix=g_mix[l], w_in=w_in[l], b_in=b_in[l], w_dw=w_dw[l], b_dw=b_dw[l],
                 ln_g=ln_g[l], ln_b=ln_b[l], w_conv_out=w_conv_out[l],
                 ssm_a_re=ssm_a_re[l], ssm_a_im=ssm_a_im[l], ssm_log_dt=ssm_log_dt[l],
                 ssm_b_re=ssm_b_re[l], ssm_b_im=ssm_b_im[l], ssm_c_re=ssm_c_re[l], ssm_c_im=ssm_c_im[l],
                 ssm_d=ssm_d[l], w_glu=w_glu[l], b_glu=b_glu[l], w_ssm_out=w_ssm_out[l],
                 q_gain=q_gain[l], k_gain=k_gain[l], w_attn_out=w_attn_out[l], w_out=w_out[l],
                 g_ffn=g_ffn[l], w_route_g=w_route_g[l], b_route_g=b_route_g[l],
                 w_route_e=w_route_e[l], b_route_e=b_route_e[l],
                 w_e_gate=w_e_gate[l], w_e_up=w_e_up[l], w_e_down=w_e_down[l])
        xp, st = mixer_sublayer(xp, p, None, None, None)
        xp = moe_sublayer(xp, p)
        conv_p.append(st[0]); ssm_p.append(st[1]); k_p.append(st[2]); v_p.append(st[3]); lf_p.append(st[4])
        xs, st = mixer_sublayer(xs, p, cache_conv[l], state_ssm[l], (cache_k[l], cache_v[l], cache_logf[l]))
        xs = moe_sublayer(xs, p)
        conv_s.append(st[0]); ssm_s.append(st[1]); k_s.append(st[2]); v_s.append(st[3]); lf_s.append(st[4])
    return (xp, xs,
            jnp.stack(conv_p), jnp.stack(ssm_p), jnp.stack(k_p), jnp.stack(v_p), jnp.stack(lf_p),
            jnp.stack(conv_s), jnp.stack(ssm_s), jnp.stack(k_s), jnp.stack(v_s), jnp.stack(lf_s))
```

```python
import functools
import math

import numpy as np
import jax
import jax.numpy as jnp
from jax import lax
from jax.experimental import pallas as pl
from jax.experimental.pallas import tpu as pltpu

F32 = jnp.float32
BF16 = jnp.bfloat16
EPS = 1e-6
NEG = -1e30
LANES = 128
CONV_HALO = 32
SSM_CHUNK = 16
MXU_DIM = 256
BIAS_SPLIT = 3
VMEM_LIMIT_BYTES = 56 * 1024 * 1024


def _cparams(*sem):
    return pltpu.CompilerParams(dimension_semantics=sem, vmem_limit_bytes=VMEM_LIMIT_BYTES)


def _tile(n, pref, mult=8):
    if n <= pref:
        return n
    t = (pref // mult) * mult
    while t >= mult:
        if n % t == 0:
            return t
        t -= mult
    return n


def _const_spec(shape):
    nd = len(shape)
    return pl.BlockSpec(shape, lambda *_: (0,) * nd)


def _in_proj_kernel(x_ref, g_ref, w_ref, b_ref, wf_ref, bf_ref, z_ref, lf_ref, xn_ref):
    @pl.when(pl.program_id(1) == 0)
    def _():
        x = x_ref[...]
        r = lax.rsqrt(jnp.mean(x * x, axis=-1, keepdims=True) + EPS)
        xn = (x * r * g_ref[...]).astype(BF16)
        xn_ref[...] = xn
        zf = jnp.dot(xn, wf_ref[...], preferred_element_type=F32) + bf_ref[...]
        lf_ref[...] = jnp.minimum(zf, 0.0) - jnp.log(1.0 + jnp.exp(-jnp.abs(zf)))

    z = jnp.dot(xn_ref[...], w_ref[...], preferred_element_type=F32) + b_ref[...]
    z_ref[...] = z.astype(z_ref.dtype)


def _in_proj(x, g, w, b, wf, bf):
    T, D = x.shape
    N = w.shape[1]
    tm = _tile(T, 1024)
    tn = _tile(N, 1536, LANES)
    return pl.pallas_call(
        _in_proj_kernel,
        grid=(T // tm, N // tn),
        in_specs=[
            pl.BlockSpec((tm, D), lambda i, j: (i, 0)),
            pl.BlockSpec((1, D), lambda i, j: (0, 0)),
            pl.BlockSpec((D, tn), lambda i, j: (0, j)),
            pl.BlockSpec((1, tn), lambda i, j: (0, j)),
            pl.BlockSpec((D, LANES), lambda i, j: (0, 0)),
            pl.BlockSpec((1, LANES), lambda i, j: (0, 0)),
        ],
        out_specs=[
            pl.BlockSpec((tm, tn), lambda i, j: (i, j)),
            pl.BlockSpec((tm, LANES), lambda i, j: (i, 0)),
        ],
        out_shape=[jax.ShapeDtypeStruct((T, N), BF16), jax.ShapeDtypeStruct((T, LANES), F32)],
        scratch_shapes=[pltpu.VMEM((tm, D), BF16)],
        compiler_params=_cparams("parallel", "arbitrary"),
        name="in_proj",
    )(x, g, w, b, wf, bf)


def _conv_kernel(a_ref, b_ref, buf0_ref, w_ref, bdw_ref, lng_ref, lnb_ref, h_ref, nc_ref, u_ref, acc_ref,
                 *, tt, C, K):
    ti = pl.program_id(1)

    @pl.when(ti == 0)
    def _():
        u_ref[0:CONV_HALO, :] = buf0_ref[0]

    @pl.when(ti > 0)
    def _():
        u_ref[0:CONV_HALO, :] = u_ref[tt:tt + CONV_HALO, :]

    u_ref[CONV_HALO:CONV_HALO + tt, :] = a_ref[...].astype(F32) * jax.nn.sigmoid(b_ref[...].astype(F32))
    base = CONV_HALO - (K - 1)
    rc = min(tt, 128)
    for r0 in range(0, tt, rc):
        for c0 in range(0, C, LANES):
            acc = jnp.zeros((rc, LANES), F32)
            for j in range(K):
                acc = acc + w_ref[j:j + 1, c0:c0 + LANES] * u_ref[r0 + base + j:r0 + base + j + rc, c0:c0 + LANES]
            acc_ref[r0:r0 + rc, c0:c0 + LANES] = acc
    h = acc_ref[...] + bdw_ref[...]
    mu = jnp.mean(h, axis=-1, keepdims=True)
    d = h - mu
    var = jnp.mean(d * d, axis=-1, keepdims=True)
    y = d * lax.rsqrt(var + EPS) * lng_ref[...] + lnb_ref[...]
    h_ref[...] = (y * jax.nn.sigmoid(y)).astype(h_ref.dtype)

    @pl.when(ti == pl.num_programs(1) - 1)
    def _():
        nc_ref[0] = u_ref[tt:tt + CONV_HALO, :]


def _conv_branch(z, col0, B, L, buf0, w_dw, b_dw, ln_g, ln_b):
    K, C = w_dw.shape
    assert K - 1 <= CONV_HALO and L >= CONV_HALO and C % LANES == 0 and col0 % C == 0
    tt = _tile(L, 256)
    nt = L // tt
    cb = col0 // C
    wpad = jnp.zeros((CONV_HALO, C), F32).at[:K].set(w_dw)
    kern = functools.partial(_conv_kernel, tt=tt, C=C, K=K)
    return pl.pallas_call(
        kern,
        grid=(B, nt),
        in_specs=[
            pl.BlockSpec((tt, C), lambda b, t: (b * nt + t, cb)),
            pl.BlockSpec((tt, C), lambda b, t: (b * nt + t, cb + 1)),
            pl.BlockSpec((1, CONV_HALO, C), lambda b, t: (b, 0, 0)),
            _const_spec((CONV_HALO, C)),
            _const_spec((1, C)), _const_spec((1, C)), _const_spec((1, C)),
        ],
        out_specs=[
            pl.BlockSpec((tt, C), lambda b, t: (b * nt + t, 0)),
            pl.BlockSpec((1, CONV_HALO, C), lambda b, t: (b, 0, 0)),
        ],
        out_shape=[jax.ShapeDtypeStruct((B * L, C), BF16), jax.ShapeDtypeStruct((B, CONV_HALO, C), F32)],
        scratch_shapes=[pltpu.VMEM((CONV_HALO + tt, C), F32), pltpu.VMEM((tt, C), F32)],
        compiler_params=_cparams("parallel", "arbitrary"),
        name="conv_branch",
    )(z, z, buf0, wpad, b_dw[None], ln_g[None], ln_b[None])


def _ssm_matrices(a_re, a_im, log_dt, b_re, b_im, c_re, c_im):
    hp = lax.Precision.HIGHEST
    G, P, gw = b_re.shape
    S = SSM_CHUNK
    lam_re = jnp.minimum(a_re, -1e-4)
    lam_im = a_im
    dt = jnp.exp(log_dt)[:, None]
    d = jnp.arange(S + 1, dtype=F32)[None, :, None]
    mag = jnp.exp(lam_re[:, None, :] * dt[:, None, :] * d)
    ang = lam_im[:, None, :] * dt[:, None, :] * d
    pw_re, pw_im = mag * jnp.cos(ang), mag * jnp.sin(ang)
    den = lam_re * lam_re + lam_im * lam_im
    xr, xi = pw_re[:, 1] - 1.0, pw_im[:, 1]
    cf_re = (xr * lam_re + xi * lam_im) / den
    cf_im = (xi * lam_re - xr * lam_im) / den
    bb_re = cf_re[..., None] * b_re - cf_im[..., None] * b_im
    bb_im = cf_re[..., None] * b_im + cf_im[..., None] * b_re
    cp_re = c_re[:, None] * pw_re[:, :, None, :] - c_im[:, None] * pw_im[:, :, None, :]
    cp_im = c_re[:, None] * pw_im[:, :, None, :] + c_im[:, None] * pw_re[:, :, None, :]
    kk = (jnp.einsum('gdcp,gpe->gdce', cp_re[:, :S], bb_re, precision=hp)
          - jnp.einsum('gdcp,gpe->gdce', cp_im[:, :S], bb_im, precision=hp))
    s_idx = jnp.arange(S)[:, None]
    t_idx = jnp.arange(S)[None, :]
    lag = jnp.clip(t_idx - s_idx, 0, S - 1)
    m = kk[:, lag] * (t_idx >= s_idx)[None, :, :, None, None]
    m = m.transpose(0, 1, 4, 2, 3).reshape(G, S * gw, S * gw)
    rev_re = pw_re[:, S - 1 - jnp.arange(S)]
    rev_im = pw_im[:, S - 1 - jnp.arange(S)]
    p_re = rev_re[:, :, None, :] * bb_re.transpose(0, 2, 1)[:, None] - rev_im[:, :, None, :] * bb_im.transpose(0, 2, 1)[:, None]
    p_im = rev_re[:, :, None, :] * bb_im.transpose(0, 2, 1)[:, None] + rev_im[:, :, None, :] * bb_re.transpose(0, 2, 1)[:, None]
    p_re = p_re.reshape(G, S * gw, P)
    p_im = p_im.reshape(G, S * gw, P)
    w1 = jnp.concatenate([m, p_re, p_im, p_im, p_re], axis=-1)
    q_re = cp_re[:, 1:].transpose(0, 3, 1, 2).reshape(G, P, S * gw)
    q_im = -cp_im[:, 1:].transpose(0, 3, 1, 2).reshape(G, P, S * gw)
    q = jnp.concatenate([q_re, q_im], axis=1)
    ar, ai = pw_re[:, S], pw_im[:, S]
    a = jnp.zeros((G, 8, 2 * P), F32)
    a = a.at[:, 0].set(jnp.concatenate([ar, ar], -1))
    a = a.at[:, 1].set(jnp.concatenate([-ai, ai], -1))
    a = a.at[:, 2].set(jnp.concatenate([ai, -ai], -1))
    return w1.astype(BF16), q.astype(BF16), a


def _ssm_kernel(u_ref, w1_ref, q_ref, a_ref, h0_ref, h0s_ref, y_ref, hf_ref, tmp_ref, hh_ref, *, nk, B, CW, P2):
    tmp_ref[...] = jnp.dot(u_ref[0], w1_ref[0], preferred_element_type=F32)
    a1 = a_ref[0, 0:1, :]
    a2 = a_ref[0, 1:2, :]
    a2s = a_ref[0, 2:3, :]

    def body(k, carry):
        h, hs = carry
        r = pl.multiple_of(k * B, B)
        hh_ref[pl.ds(r, B), :] = h
        inc = tmp_ref[pl.ds(r, B), CW:CW + P2]
        incs = tmp_ref[pl.ds(r, B), CW + P2:CW + 2 * P2]
        return a1 * h + a2 * hs + inc, a1 * hs + a2s * h + incs

    h, _ = lax.fori_loop(0, nk, body, (h0_ref[0], h0s_ref[0]))
    hf_ref[0] = h
    y = tmp_ref[:, 0:CW] + jnp.dot(hh_ref[...].astype(BF16), q_ref[0], preferred_element_type=F32)
    y_ref[0] = y.astype(y_ref.dtype)


def _ssm_branch(u, B, L, mats, h0):
    w1, q, a = mats
    G, CW, _ = w1.shape
    P2 = q.shape[1]
    gw = CW // SSM_CHUNK
    assert CW == MXU_DIM and L % SSM_CHUNK == 0
    nk = L // SSM_CHUNK
    R = nk * B
    ug = u.reshape(B, nk, SSM_CHUNK, G, gw).transpose(3, 1, 0, 2, 4).reshape(G, R, CW)
    h0r = jnp.concatenate([h0[..., 0], h0[..., 1]], -1).transpose(1, 0, 2)
    h0s = jnp.concatenate([h0[..., 1], h0[..., 0]], -1).transpose(1, 0, 2)
    kern = functools.partial(_ssm_kernel, nk=nk, B=B, CW=CW, P2=P2)
    y, hf = pl.pallas_call(
        kern,
        grid=(G,),
        in_specs=[
            pl.BlockSpec((1, R, CW), lambda g: (g, 0, 0)),
            pl.BlockSpec((1, CW, CW + 2 * P2), lambda g: (g, 0, 0)),
            pl.BlockSpec((1, P2, CW), lambda g: (g, 0, 0)),
            pl.BlockSpec((1, 8, P2), lambda g: (g, 0, 0)),
            pl.BlockSpec((1, B, P2), lambda g: (g, 0, 0)),
            pl.BlockSpec((1, B, P2), lambda g: (g, 0, 0)),
        ],
        out_specs=[
            pl.BlockSpec((1, R, CW), lambda g: (g, 0, 0)),
            pl.BlockSpec((1, B, P2), lambda g: (g, 0, 0)),
        ],
        out_shape=[jax.ShapeDtypeStruct((G, R, CW), BF16), jax.ShapeDtypeStruct((G, B, P2), F32)],
        scratch_shapes=[pltpu.VMEM((R, CW + 2 * P2), F32), pltpu.VMEM((R, P2), F32)],
        compiler_params=_cparams("parallel"),
        name="ssm_branch",
    )(ug, w1, q, a, h0r, h0s)
    y = y.reshape(G, nk, B, SSM_CHUNK, gw).transpose(2, 1, 3, 0, 4).reshape(B * L, G * gw)
    P = P2 // 2
    new_state = jnp.stack([hf[..., :P], hf[..., P:]], axis=-1).transpose(1, 0, 2, 3)
    return y, new_state


def _bias_tables(H):
    assert H % 2 == 0 and BIAS_SPLIT * H <= LANES
    npair = H // 2
    pq = np.zeros((LANES, npair * LANES), np.float32)
    pk = np.zeros((LANES, npair * LANES), np.float32)
    oq = np.zeros((1, npair * LANES), np.float32)
    ok = np.zeros((1, npair * LANES), np.float32)
    for h in range(H):
        p, e = divmod(h, 2)
        for i in range(BIAS_SPLIT):
            pq[i * H + h, p * LANES + e * 2 * BIAS_SPLIT + i] = 1.0
            ok[0, p * LANES + e * 2 * BIAS_SPLIT + i] = 1.0
            pk[i * H + h, p * LANES + e * 2 * BIAS_SPLIT + BIAS_SPLIT + i] = -1.0
            oq[0, p * LANES + e * 2 * BIAS_SPLIT + BIAS_SPLIT + i] = 1.0
    return jnp.asarray(pq, BF16), jnp.asarray(pk, BF16), jnp.asarray(oq), jnp.asarray(ok)


def _split3(x):
    hi = x.astype(BF16)
    r1 = x - hi.astype(F32)
    mid = r1.astype(BF16)
    lo = (r1 - mid.astype(F32)).astype(BF16)
    return hi, mid, lo


def _forget_cumsum(lf, carry, H):
    tl = lf.shape[0]
    lane = lax.broadcasted_iota(jnp.int32, (1, LANES), 1)
    lf = jnp.where(lane < H, lf, 0.0)
    tri = (lax.broadcasted_iota(jnp.int32, (tl, tl), 0) >= lax.broadcasted_iota(jnp.int32, (tl, tl), 1)).astype(BF16)
    hi, mid, lo = _split3(lf)
    c = (jnp.dot(tri, hi, preferred_element_type=F32) + jnp.dot(tri, mid, preferred_element_type=F32)
         + jnp.dot(tri, lo, preferred_element_type=F32))
    return c + carry


def _pack_split(c, H):
    hi, mid, lo = _split3(c)
    packed = hi.astype(F32) + pltpu.roll(mid.astype(F32), H, axis=1) + pltpu.roll(lo.astype(F32), 2 * H, axis=1)
    return packed.astype(BF16)


def _pair_rmsnorm(x, gain, lo_half, hd):
    sq = x * x
    s_lo = jnp.sum(jnp.where(lo_half, sq, 0.0), axis=-1, keepdims=True)
    s_hi = jnp.sum(jnp.where(lo_half, 0.0, sq), axis=-1, keepdims=True)
    ms = jnp.where(lo_half, s_lo, s_hi) * (1.0 / hd)
    return x * lax.rsqrt(ms + EPS) * gain


def _qk_kernel(zq_ref, zk_ref, zv_ref, lf_ref, qg_ref, kg_ref, c0_ref, pq_ref, pk_ref, oq_ref, ok_ref,
               qa_ref, ka_ref, kf_ref, vf_ref, c_ref, *, tl, H, hd, scale):
    @pl.when(pl.program_id(1) == 0)
    def _():
        c_ref[...] = c0_ref[0]

    c = _forget_cumsum(lf_ref[...], c_ref[0:1, :], H)
    c_ref[...] = jnp.broadcast_to(c[tl - 1:tl, :], c_ref.shape)
    cc = _pack_split(c, H)
    bq = jnp.dot(cc, pq_ref[...], preferred_element_type=F32) + oq_ref[...]
    bk = jnp.dot(cc, pk_ref[...], preferred_element_type=F32) + ok_ref[...]
    lo_half = lax.broadcasted_iota(jnp.int32, (1, LANES), 1) < hd
    for p in range(H // 2):
        sl = slice(p * LANES, (p + 1) * LANES)
        qn = _pair_rmsnorm(zq_ref[:, sl].astype(F32), qg_ref[:, sl], lo_half, hd) * scale
        kn = _pair_rmsnorm(zk_ref[:, sl].astype(F32), kg_ref[:, sl], lo_half, hd)
        qa_ref[:, 2 * p * LANES:(2 * p + 1) * LANES] = qn.astype(BF16)
        qa_ref[:, (2 * p + 1) * LANES:(2 * p + 2) * LANES] = bq[:, sl].astype(BF16)
        ka_ref[:, 2 * p * LANES:(2 * p + 1) * LANES] = kn.astype(BF16)
        ka_ref[:, (2 * p + 1) * LANES:(2 * p + 2) * LANES] = bk[:, sl].astype(BF16)
        kf_ref[:, sl] = kn
    vf_ref[...] = zv_ref[...].astype(F32)


def _qk_prep(z, qcol, B, L, logf, q_gain, k_gain, c0, H):
    hd = q_gain.shape[0]
    AW = H * hd
    assert 2 * hd == LANES and qcol % AW == 0
    tl = _tile(L, 256)
    nt = L // tl
    qb = qcol // AW
    pq, pk, oq, ok = _bias_tables(H)
    NA = H * LANES
    kern = functools.partial(_qk_kernel, tl=tl, H=H, hd=hd, scale=hd ** -0.5)
    row = lambda b, t: b * nt + t
    return pl.pallas_call(
        kern,
        grid=(B, nt),
        in_specs=[
            pl.BlockSpec((tl, AW), lambda b, t: (row(b, t), qb)),
            pl.BlockSpec((tl, AW), lambda b, t: (row(b, t), qb + 1)),
            pl.BlockSpec((tl, AW), lambda b, t: (row(b, t), qb + 2)),
            pl.BlockSpec((tl, LANES), lambda b, t: (row(b, t), 0)),
            _const_spec((1, AW)), _const_spec((1, AW)),
            pl.BlockSpec((1, 8, LANES), lambda b, t: (b, 0, 0)),
            _const_spec(pq.shape), _const_spec(pk.shape), _const_spec(oq.shape), _const_spec(ok.shape),
        ],
        out_specs=[
            pl.BlockSpec((tl, NA), lambda b, t: (row(b, t), 0)),
            pl.BlockSpec((tl, NA), lambda b, t: (row(b, t), 0)),
            pl.BlockSpec((tl, AW), lambda b, t: (row(b, t), 0)),
            pl.BlockSpec((tl, AW), lambda b, t: (row(b, t), 0)),
        ],
        out_shape=[jax.ShapeDtypeStruct((B * L, NA), BF16), jax.ShapeDtypeStruct((B * L, NA), BF16),
                   jax.ShapeDtypeStruct((B * L, AW), F32), jax.ShapeDtypeStruct((B * L, AW), F32)],
        scratch_shapes=[pltpu.VMEM((8, LANES), F32)],
        compiler_params=_cparams("parallel", "arbitrary"),
        name="qk_prep",
    )(z, z, z, logf, jnp.tile(q_gain, H)[None], jnp.tile(k_gain, H)[None], c0, pq, pk, oq, ok)


def _past_kernel(k_ref, v_ref, lf_ref, pk_ref, ok_ref, ka_ref, vb_ref, ct_ref, c_ref, *, tl, H):
    @pl.when(pl.program_id(1) == 0)
    def _():
        c_ref[...] = jnp.zeros_like(c_ref)

    c = _forget_cumsum(lf_ref[...], c_ref[0:1, :], H)
    c_ref[...] = jnp.broadcast_to(c[tl - 1:tl, :], c_ref.shape)
    bk = jnp.dot(_pack_split(c, H), pk_ref[...], preferred_element_type=F32) + ok_ref[...]
    for p in range(H // 2):
        sl = slice(p * LANES, (p + 1) * LANES)
        ka_ref[:, 2 * p * LANES:(2 * p + 1) * LANES] = k_ref[:, sl].astype(BF16)
        ka_ref[:, (2 * p + 1) * LANES:(2 * p + 2) * LANES] = bk[:, sl].astype(BF16)
    vb_ref[...] = v_ref[...].astype(BF16)
    ct_ref[0] = c_ref[...]


def _past_prep(k_past, v_past, lf_past, H):
    B, PL, AW = k_past.shape
    tl = _tile(PL, 256)
    nt = PL // tl
    _, pk, _, ok = _bias_tables(H)
    NA = H * LANES
    row = lambda b, t: b * nt + t
    return pl.pallas_call(
        functools.partial(_past_kernel, tl=tl, H=H),
        grid=(B, nt),
        in_specs=[
            pl.BlockSpec((tl, AW), lambda b, t: (row(b, t), 0)),
            pl.BlockSpec((tl, AW), lambda b, t: (row(b, t), 0)),
            pl.BlockSpec((tl, LANES), lambda b, t: (row(b, t), 0)),
            _const_spec(pk.shape), _const_spec(ok.shape),
        ],
        out_specs=[
            pl.BlockSpec((tl, NA), lambda b, t: (row(b, t), 0)),
            pl.BlockSpec((tl, AW), lambda b, t: (row(b, t), 0)),
            pl.BlockSpec((1, 8, LANES), lambda b, t: (b, 0, 0)),
        ],
        out_shape=[jax.ShapeDtypeStruct((B * PL, NA), BF16), jax.ShapeDtypeStruct((B * PL, AW), BF16),
                   jax.ShapeDtypeStruct((B, 8, LANES), F32)],
        scratch_shapes=[pltpu.VMEM((8, LANES), F32)],
        compiler_params=_cparams("parallel", "arbitrary"),
        name="past_prep",
    )(k_past.reshape(B * PL, AW), v_past.reshape(B * PL, AW), lf_past.reshape(B * PL, LANES), pk, ok)


def _attn_kernel(q_ref, k_ref, v_ref, o_ref, acc_ref, m_ref, l_ref, *, tq, tk, q_off, hd):
    qi = pl.program_id(2)
    q = q_ref[...]
    lane = lax.broadcasted_iota(jnp.int32, (1, 2 * LANES), 1)
    q_lo = q_off + qi * tq
    n_full = (q_lo + 1) // tk
    n_tot = (q_lo + tq + tk - 1) // tk
    nb = 2 * BIAS_SPLIT
    outs = []
    for e in range(2):
        msk = ((lane >= e * hd) & (lane < (e + 1) * hd)) | ((lane >= LANES + e * nb) & (lane < LANES + (e + 1) * nb))
        qe = jnp.where(msk, q, jnp.zeros_like(q))
        m_ref[...] = jnp.full(m_ref.shape, NEG, F32)
        l_ref[...] = jnp.zeros(l_ref.shape, F32)
        acc_ref[...] = jnp.zeros(acc_ref.shape, F32)

        def step(j, masked):
            k0 = pl.multiple_of(j * tk, tk)
            ks = k_ref[pl.ds(k0, tk), :]
            vs = v_ref[pl.ds(k0, tk), :]
            s = lax.dot_general(qe, ks, (((1,), (1,)), ((), ())), preferred_element_type=F32)
            if masked:
                kpos = k0 + lax.broadcasted_iota(jnp.int32, (1, tk), 1)
                qpos = q_lo + lax.broadcasted_iota(jnp.int32, (tq, 1), 0)
                s = jnp.where(kpos <= qpos, s, NEG)
            m_old = m_ref[...]
            m_new = jnp.maximum(m_old, jnp.max(s, axis=-1, keepdims=True))
            alpha = jnp.exp(m_old - m_new)
            p = jnp.exp(s - m_new)
            l_ref[...] = alpha * l_ref[...] + jnp.sum(p, axis=-1, keepdims=True)
            acc_ref[...] = alpha * acc_ref[...] + jnp.dot(p.astype(BF16), vs, preferred_element_type=F32)
            m_ref[...] = m_new

        def full_body(j, c):
            step(j, False)
            return c

        def edge_body(j, c):
            step(j, True)
            return c

        lax.fori_loop(0, n_full, full_body, 0)
        lax.fori_loop(n_full, n_tot, edge_body, 0)
        outs.append(acc_ref[...] / l_ref[...])
    lane1 = lax.broadcasted_iota(jnp.int32, (1, LANES), 1)
    o_ref[...] = jnp.where(lane1 < hd, outs[0], outs[1]).astype(o_ref.dtype)


def _attention(qa, ka, v, vcol, B, Lq, Lk, q_off, H, hd, tq, tk):
    npair = H // 2
    nq = Lq // tq
    assert Lq % tq == 0 and Lk % tk == 0 and vcol % LANES == 0 and Lk >= q_off + Lq
    vb = vcol // LANES
    kern = functools.partial(_attn_kernel, tq=tq, tk=tk, q_off=q_off, hd=hd)
    return pl.pallas_call(
        kern,
        grid=(B, npair, nq),
        in_specs=[
            pl.BlockSpec((tq, 2 * LANES), lambda b, p, i: (b * nq + i, p)),
            pl.BlockSpec((Lk, 2 * LANES), lambda b, p, i: (b, p)),
            pl.BlockSpec((Lk, LANES), lambda b, p, i: (b, vb + p)),
        ],
        out_specs=pl.BlockSpec((tq, LANES), lambda b, p, i: (b * nq + i, p)),
        out_shape=jax.ShapeDtypeStruct((B * Lq, H * hd), BF16),
        scratch_shapes=[pltpu.VMEM((tq, LANES), F32), pltpu.VMEM((tq, 1), F32), pltpu.VMEM((tq, 1), F32)],
        compiler_params=_cparams("parallel", "parallel", "arbitrary"),
        name="fox_attention",
    )(qa, ka, v)


def _merge_kernel(x_ref, ha_ref, ys_ref, u_ref, o_ref, ga_ref, gb_ref, gc_ref, d_ref, wglu_ref, bglu_ref,
                  wa_ref, wb_ref, wc_ref, wo_ref, out_ref):
    ys = ys_ref[...].astype(F32) + d_ref[...] * u_ref[...].astype(F32)
    hs = 0.5 * ys * (1.0 + jnp.tanh(math.sqrt(2.0 / math.pi) * (ys + 0.044715 * (ys * ys * ys))))
    gl = jnp.dot(hs.astype(BF16), wglu_ref[...], preferred_element_type=F32) + bglu_ref[...]
    hs = hs * jax.nn.sigmoid(gl)
    oa = jnp.dot(ha_ref[...], wa_ref[...], preferred_element_type=F32)
    ob = jnp.dot(hs.astype(BF16), wb_ref[...], preferred_element_type=F32)
    oc = jnp.dot(o_ref[...], wc_ref[...], preferred_element_type=F32)
    m = (jax.nn.sigmoid(ga_ref[...].astype(F32)) * oa + jax.nn.sigmoid(gb_ref[...].astype(F32)) * ob
         + jax.nn.sigmoid(gc_ref[...].astype(F32)) * oc)
    out_ref[...] = x_ref[...] + jnp.dot(m.astype(BF16), wo_ref[...], preferred_element_type=F32)


def _merge(x, ha, ys, z, ucol, gcol, o, d, wglu, bglu, wa, wb, wc, wo):
    T, D = x.shape
    C, S, AW = ha.shape[1], ys.shape[1], o.shape[1]
    assert ucol % S == 0 and gcol % D == 0
    tm = _tile(T, 256)
    ub, gb = ucol // S, gcol // D
    single = pl.Buffered(1)
    wspec = lambda a: pl.BlockSpec(a.shape, lambda i: (0, 0), pipeline_mode=single)
    return pl.pallas_call(
        _merge_kernel,
        grid=(T // tm,),
        in_specs=[
            pl.BlockSpec((tm, D), lambda i: (i, 0)),
            pl.BlockSpec((tm, C), lambda i: (i, 0)),
            pl.BlockSpec((tm, S), lambda i: (i, 0)),
            pl.BlockSpec((tm, S), lambda i: (i, ub)),
            pl.BlockSpec((tm, AW), lambda i: (i, 0)),
            pl.BlockSpec((tm, D), lambda i: (i, gb)),
            pl.BlockSpec((tm, D), lambda i: (i, gb + 1)),
            pl.BlockSpec((tm, D), lambda i: (i, gb + 2)),
            _const_spec((1, S)), wspec(wglu), _const_spec((1, S)),
            wspec(wa), wspec(wb), wspec(wc), wspec(wo),
        ],
        out_specs=pl.BlockSpec((tm, D), lambda i: (i, 0)),
        out_shape=jax.ShapeDtypeStruct((T, D), F32),
        compiler_params=_cparams("parallel"),
        name="merge",
    )(x, ha, ys, z, o, z, z, z, d[None], wglu, bglu[None], wa, wb, wc, wo)


def _route_kernel(x_ref, g_ref, wh_ref, wl_ref, b_ref, r_ref, *, ng, epg):
    x = x_ref[...]
    xn = x * lax.rsqrt(jnp.mean(x * x, axis=-1, keepdims=True) + EPS) * g_ref[...]
    xh = xn.astype(BF16)
    xl = (xn - xh.astype(F32)).astype(BF16)
    wh = wh_ref[...]
    lg = (jnp.dot(xh, wh, preferred_element_type=F32) + jnp.dot(xl, wh, preferred_element_type=F32)
          + jnp.dot(xh, wl_ref[...], preferred_element_type=F32) + b_ref[...])
    lane = lax.broadcasted_iota(jnp.int32, (1, LANES), 1).astype(F32)
    big = float(LANES)
    is_g = lane < ng
    gmax = jnp.max(jnp.where(is_g, lg, NEG), axis=-1, keepdims=True)
    den = jnp.sum(jnp.where(is_g, jnp.exp(jnp.where(is_g, lg, NEG) - gmax), 0.0), axis=-1, keepdims=True)
    p_top = 1.0 / den
    g_top = jnp.min(jnp.where(is_g & (lg == gmax), lane, big), axis=-1, keepdims=True)
    sel = (lane >= ng + g_top * epg) & (lane < ng + (g_top + 1.0) * epg)
    le = jnp.where(sel, lg, NEG)
    v1 = jnp.max(le, axis=-1, keepdims=True)
    i1 = jnp.min(jnp.where(sel & (le == v1), lane, big), axis=-1, keepdims=True)
    sel2 = sel & (lane != i1)
    le2 = jnp.where(sel2, lg, NEG)
    v2 = jnp.max(le2, axis=-1, keepdims=True)
    i2 = jnp.min(jnp.where(sel2 & (le2 == v2), lane, big), axis=-1, keepdims=True)
    t = jnp.exp(v2 - v1)
    w1 = p_top / (1.0 + t)
    w2 = p_top * t / (1.0 + t)
    r_ref[...] = jnp.where(lane == 0, i1 - ng, jnp.where(lane == 1, i2 - ng, jnp.where(lane == 2, w1, jnp.where(lane == 3, w2, 0.0))))


def _route(x, g, w_route_g, b_route_g, w_route_e, b_route_e):
    T, D = x.shape
    ng, ne = w_route_g.shape[1], w_route_e.shape[1]
    assert ng + ne <= LANES
    w = jnp.zeros((D, LANES), F32).at[:, :ng].set(w_route_g).at[:, ng:ng + ne].set(w_route_e)
    b = jnp.zeros((1, LANES), F32).at[0, :ng].set(b_route_g).at[0, ng:ng + ne].set(b_route_e)
    wh = w.astype(BF16)
    wl = (w - wh.astype(F32)).astype(BF16)
    tm = _tile(T, 512)
    return pl.pallas_call(
        functools.partial(_route_kernel, ng=ng, epg=ne // ng),
        grid=(T // tm,),
        in_specs=[pl.BlockSpec((tm, D), lambda i: (i, 0)), _const_spec((1, D)),
                  _const_spec((D, LANES)), _const_spec((D, LANES)), _const_spec((1, LANES))],
        out_specs=pl.BlockSpec((tm, LANES), lambda i: (i, 0)),
        out_shape=jax.ShapeDtypeStruct((T, LANES), F32),
        compiler_params=_cparams("parallel"),
        name="route",
    )(x, g[None], wh, wl, b)


def _dispatch_plan(route, n_exp, tmo):
    T = route.shape[0]
    eid = route[:, 0:2].astype(jnp.int32).reshape(-1)
    wgt = route[:, 2:4].reshape(-1)
    onehot = (eid[:, None] == jnp.arange(n_exp)[None, :]).astype(jnp.int32)
    csum = jnp.cumsum(onehot, axis=0)
    rank = jnp.sum((csum - onehot) * onehot, axis=1)
    counts = csum[-1]
    padded = ((counts + tmo - 1) // tmo) * tmo
    ends = jnp.cumsum(padded)
    starts = ends - padded
    pos = starts[eid] + rank
    NP = 2 * T + n_exp * tmo
    tok = jnp.zeros((NP,), jnp.int32).at[pos].set(jnp.arange(2 * T, dtype=jnp.int32) // 2)
    wrow = jnp.zeros((NP,), F32).at[pos].set(wgt)
    tile_start = jnp.arange(NP // tmo, dtype=jnp.int32) * tmo
    tile_exp = jnp.minimum(jnp.searchsorted(ends, tile_start, side='right'), n_exp - 1).astype(jnp.int32)
    n_used = (ends[-1] // tmo).astype(jnp.int32).reshape(1)
    return tok, wrow[:, None], tile_exp, n_used, pos.astype(jnp.int32)


def _moe_kernel(te_ref, tok_ref, nu_ref, x_hbm, g_ref, w_ref, wg_ref, wu_ref, wd_ref, y_ref, xbuf, sem, *, tmo):
    i = pl.program_id(0)
    n_used = nu_ref[0]

    def gather(tile, slot):
        def issue(r, c):
            t = tok_ref[tile * tmo + r]
            pltpu.make_async_copy(x_hbm.at[pl.ds(t, 1), :], xbuf.at[slot, pl.ds(r, 1), :], sem.at[slot]).start()
            return c
        lax.fori_loop(0, tmo, issue, 0)

    @pl.when(i == 0)
    def _():
        gather(0, 0)

    @pl.when(i + 1 < n_used)
    def _():
        gather(i + 1, (i + 1) % 2)

    @pl.when(i < n_used)
    def _():
        slot = i % 2
        pltpu.make_async_copy(x_hbm.at[pl.ds(0, tmo), :], xbuf.at[slot], sem.at[slot]).wait()
        x = xbuf[slot]
        xn = (x * lax.rsqrt(jnp.mean(x * x, axis=-1, keepdims=True) + EPS) * g_ref[...]).astype(BF16)
        hg = jnp.dot(xn, wg_ref[0], preferred_element_type=F32)
        hu = jnp.dot(xn, wu_ref[0], preferred_element_type=F32)
        h = (hg * jax.nn.sigmoid(hg) * hu).astype(BF16)
        y_ref[...] = jnp.dot(h, wd_ref[0], preferred_element_type=F32) * w_ref[...]

    @pl.when(i >= n_used)
    def _():
        y_ref[...] = jnp.zeros_like(y_ref)


def _moe(x, g, plan, wg, wu, wd, tmo):
    T, D = x.shape
    tok, wrow, tile_exp, n_used, _ = plan
    NP = tok.shape[0]
    E, _, DE = wg.shape
    grid_spec = pltpu.PrefetchScalarGridSpec(
        num_scalar_prefetch=3,
        grid=(NP // tmo,),
        in_specs=[
            pl.BlockSpec(memory_space=pl.ANY),
            pl.BlockSpec((1, D), lambda i, te, tk, nu: (0, 0)),
            pl.BlockSpec((tmo, 1), lambda i, te, tk, nu: (i, 0)),
            pl.BlockSpec((1, D, DE), lambda i, te, tk, nu: (te[i], 0, 0)),
            pl.BlockSpec((1, D, DE), lambda i, te, tk, nu: (te[i], 0, 0)),
            pl.BlockSpec((1, DE, D), lambda i, te, tk, nu: (te[i], 0, 0)),
        ],
        out_specs=pl.BlockSpec((tmo, D), lambda i, te, tk, nu: (i, 0)),
        scratch_shapes=[pltpu.VMEM((2, tmo, D), F32), pltpu.SemaphoreType.DMA((2,))],
    )
    return pl.pallas_call(
        functools.partial(_moe_kernel, tmo=tmo),
        grid_spec=grid_spec,
        out_shape=jax.ShapeDtypeStruct((NP, D), F32),
        compiler_params=_cparams("arbitrary"),
        name="moe_experts",
    )(tile_exp, tok, n_used, x, g[None], wrow, wg, wu, wd)


def _combine_kernel(pos_ref, x_ref, y_hbm, out_ref, ybuf, sem, *, tc):
    i = pl.program_id(0)
    n = pl.num_programs(0)

    def gather(tile, slot):
        def issue(r, c):
            for s in range(2):
                p = pos_ref[2 * (tile * tc + r) + s]
                pltpu.make_async_copy(y_hbm.at[pl.ds(p, 1), :], ybuf.at[slot, s, pl.ds(r, 1), :], sem.at[slot]).start()
            return c
        lax.fori_loop(0, tc, issue, 0)

    @pl.when(i == 0)
    def _():
        gather(0, 0)

    @pl.when(i + 1 < n)
    def _():
        gather(i + 1, (i + 1) % 2)

    slot = i % 2
    for s in range(2):
        pltpu.make_async_copy(y_hbm.at[pl.ds(0, tc), :], ybuf.at[slot, s], sem.at[slot]).wait()
    out_ref[...] = x_ref[...] + (ybuf[slot, 0] + ybuf[slot, 1])


def _combine(x, y_sorted, pos):
    T, D = x.shape
    tc = _tile(T, 256)
    grid_spec = pltpu.PrefetchScalarGridSpec(
        num_scalar_prefetch=1,
        grid=(T // tc,),
        in_specs=[pl.BlockSpec((tc, D), lambda i, p: (i, 0)), pl.BlockSpec(memory_space=pl.ANY)],
        out_specs=pl.BlockSpec((tc, D), lambda i, p: (i, 0)),
        scratch_shapes=[pltpu.VMEM((2, 2, tc, D), F32), pltpu.SemaphoreType.DMA((2,))],
    )
    return pl.pallas_call(
        functools.partial(_combine_kernel, tc=tc),
        grid_spec=grid_spec,
        out_shape=jax.ShapeDtypeStruct((T, D), F32),
        compiler_params=_cparams("arbitrary"),
        name="moe_combine",
    )(pos, x, y_sorted)


def _layer_weights(p, H):
    D = p['w_in'].shape[0]
    C = p['w_dw'].shape[1]
    S = p['ssm_d'].shape[0]
    AW = H * p['q_gain'].shape[0]
    o_u = 2 * C
    o_q = o_u + S
    o_f = o_q + 3 * AW
    o_g = o_f + H
    perm = lambda a: jnp.concatenate([a[..., o_g:], a[..., o_q:o_f], a[..., :o_u], a[..., o_u:o_q]], axis=-1)
    cols = dict(g=0, q=3 * D, conv=3 * D + 3 * AW, u=3 * D + 3 * AW + 2 * C)
    wf = jnp.zeros((D, LANES), F32).at[:, :H].set(p['w_in'][:, o_f:o_g]).astype(BF16)
    bf = jnp.zeros((1, LANES), F32).at[0, :H].set(p['b_in'][o_f:o_g])
    return perm(p['w_in']).astype(BF16), perm(p['b_in'])[None], wf, bf, cols


def _mixer_and_moe(x3, p, conv_buf, ssm_h0, past, H):
    B, L, D = x3.shape
    T = B * L
    x = x3.reshape(T, D)
    hd = p['q_gain'].shape[0]
    AW = H * hd
    K, C = p['w_dw'].shape
    G, P, gw = p['ssm_b_re'].shape
    S = G * gw
    w_main, b_main, wf, bf, cols = _layer_weights(p, H)
    z, logf = _in_proj(x, p['g_mix'][None], w_main, b_main, wf, bf)

    buf0 = jnp.zeros((B, CONV_HALO, C), F32)
    if conv_buf is not None:
        buf0 = buf0.at[:, CONV_HALO - (K - 1):].set(conv_buf)
    ha, nc = _conv_branch(z, cols['conv'], B, L, buf0, p['w_dw'], p['b_dw'], p['ln_g'], p['ln_b'])
    new_conv = nc[:, CONV_HALO - (K - 1):]

    mats = _ssm_matrices(p['ssm_a_re'], p['ssm_a_im'], p['ssm_log_dt'], p['ssm_b_re'], p['ssm_b_im'],
                         p['ssm_c_re'], p['ssm_c_im'])
    h0 = jnp.zeros((B, G, P, 2), F32) if ssm_h0 is None else ssm_h0
    ys, new_ssm = _ssm_branch(z[:, cols['u']:cols['u'] + S], B, L, mats, h0)

    if past is None:
        c0 = jnp.zeros((B, 8, LANES), F32)
        qa, ka, kf, vf = _qk_prep(z, cols['q'], B, L, logf, p['q_gain'], p['k_gain'], c0, H)
        tq = _tile(L, 256)
        o = _attention(qa, ka, z, cols['q'] + 2 * AW, B, L, L, 0, H, hd, tq, tq)
    else:
        k_past, v_past, lf_past = past
        PL = k_past.shape[1]
        lfp = jnp.zeros((B, PL, LANES), F32).at[..., :H].set(lf_past)
        ka_p, vb_p, ctot = _past_prep(k_past.reshape(B, PL, AW), v_past.reshape(B, PL, AW), lfp, H)
        qa, ka, kf, vf = _qk_prep(z, cols['q'], B, L, logf, p['q_gain'], p['k_gain'], ctot, H)
        tk = _tile(PL, 256)
        Lk = ((PL + L + tk - 1) // tk) * tk
        NA = H * LANES
        ka_all = jnp.concatenate([ka_p.reshape(B, PL, NA), ka.reshape(B, L, NA),
                                  jnp.zeros((B, Lk - PL - L, NA), BF16)], axis=1).reshape(B * Lk, NA)
        v_all = jnp.concatenate([vb_p.reshape(B, PL, AW), vf.reshape(B, L, AW).astype(BF16),
                                 jnp.zeros((B, Lk - PL - L, AW), BF16)], axis=1).reshape(B * Lk, AW)
        o = _attention(qa, ka_all, v_all, 0, B, L, Lk, PL, H, hd, L, tk)

    bfw = lambda a: a.astype(BF16)
    x = _merge(x, ha, ys, z, cols['u'], cols['g'], o, p['ssm_d'], bfw(p['w_glu']), p['b_glu'],
               bfw(p['w_conv_out']), bfw(p['w_ssm_out']), bfw(p['w_attn_out']), bfw(p['w_out']))

    route = _route(x, p['g_ffn'], p['w_route_g'], p['b_route_g'], p['w_route_e'], p['b_route_e'])
    E = p['w_e_gate'].shape[0]
    tmo = _tile(T, 256)
    plan = _dispatch_plan(route, E, tmo)
    y_sorted = _moe(x, p['g_ffn'], plan, bfw(p['w_e_gate']), bfw(p['w_e_up']), bfw(p['w_e_down']), tmo)
    x = _combine(x, y_sorted, plan[4])

    state = (new_conv, new_ssm, kf.reshape(B, L, H, hd), vf.reshape(B, L, H, hd), logf[:, :H].reshape(B, L, H))
    return x.reshape(B, L, D), state


def kernel(x_prompt, x_sample, cache_conv, state_ssm, cache_k, cache_v, cache_logf, g_mix, w_in, b_in, w_dw, b_dw, ln_g, ln_b, w_conv_out, ssm_a_re, ssm_a_im, ssm_log_dt, ssm_b_re, ssm_b_im, ssm_c_re, ssm_c_im, ssm_d, w_glu, b_glu, w_ssm_out, q_gain, k_gain, w_attn_out, w_out, g_ffn, w_route_g, b_route_g, w_route_e, b_route_e, w_e_gate, w_e_up, w_e_down):
    depth = w_in.shape[0]
    H = cache_logf.shape[-1]
    stacked = dict(g_mix=g_mix, w_in=w_in, b_in=b_in, w_dw=w_dw, b_dw=b_dw, ln_g=ln_g, ln_b=ln_b,
                   w_conv_out=w_conv_out, ssm_a_re=ssm_a_re, ssm_a_im=ssm_a_im, ssm_log_dt=ssm_log_dt,
                   ssm_b_re=ssm_b_re, ssm_b_im=ssm_b_im, ssm_c_re=ssm_c_re, ssm_c_im=ssm_c_im, ssm_d=ssm_d,
                   w_glu=w_glu, b_glu=b_glu, w_ssm_out=w_ssm_out, q_gain=q_gain, k_gain=k_gain,
                   w_attn_out=w_attn_out, w_out=w_out, g_ffn=g_ffn, w_route_g=w_route_g, b_route_g=b_route_g,
                   w_route_e=w_route_e, b_route_e=b_route_e, w_e_gate=w_e_gate, w_e_up=w_e_up, w_e_down=w_e_down)
    xp, xs = x_prompt, x_sample
    st_p, st_s = [], []
    for l in range(depth):
        p = {k: v[l] for k, v in stacked.items()}
        xp, st = _mixer_and_moe(xp, p, None, None, None, H)
        st_p.append(st)
        xs, st = _mixer_and_moe(xs, p, cache_conv[l], state_ssm[l], (cache_k[l], cache_v[l], cache_logf[l]), H)
        st_s.append(st)
    stack = lambda sts, i: jnp.stack([s[i] for s in sts])
    return (xp, xs) + tuple(stack(st_p, i) for i in range(5)) + tuple(stack(st_s, i) for i in range(5))
```

```python
import functools
import math

import numpy as np
import jax
import jax.numpy as jnp
from jax import lax
from jax.experimental import pallas as pl
from jax.experimental.pallas import tpu as pltpu

F32 = jnp.float32
BF16 = jnp.bfloat16
EPS = 1e-6
NEG = -1e30
LANES = 128
CONV_HALO = 32
SSM_CHUNK = 16
MXU_DIM = 256
BIAS_SPLIT = 3
VMEM_LIMIT_BYTES = 56 * 1024 * 1024


def _cparams(*sem):
    return pltpu.CompilerParams(dimension_semantics=sem, vmem_limit_bytes=VMEM_LIMIT_BYTES)


def _tile(n, pref, mult=8):
    if n <= pref:
        return n
    t = (pref // mult) * mult
    while t >= mult:
        if n % t == 0:
            return t
        t -= mult
    return n


def _const_spec(shape):
    nd = len(shape)
    return pl.BlockSpec(shape, lambda *_: (0,) * nd)


def _in_proj_kernel(x_ref, g_ref, w_ref, b_ref, wf_ref, bf_ref, z_ref, lf_ref, xn_ref):
    @pl.when(pl.program_id(1) == 0)
    def _():
        x = x_ref[...]
        r = lax.rsqrt(jnp.mean(x * x, axis=-1, keepdims=True) + EPS)
        xn = (x * r * g_ref[...]).astype(BF16)
        xn_ref[...] = xn
        zf = jnp.dot(xn, wf_ref[...], preferred_element_type=F32) + bf_ref[...]
        lf_ref[...] = jnp.minimum(zf, 0.0) - jnp.log(1.0 + jnp.exp(-jnp.abs(zf)))

    z = jnp.dot(xn_ref[...], w_ref[...], preferred_element_type=F32) + b_ref[...]
    z_ref[...] = z.astype(z_ref.dtype)


def _in_proj(x, g, w, b, wf, bf):
    T, D = x.shape
    N = w.shape[1]
    tm = _tile(T, 1024)
    tn = _tile(N, 1536, LANES)
    return pl.pallas_call(
        _in_proj_kernel,
        grid=(T // tm, N // tn),
        in_specs=[
            pl.BlockSpec((tm, D), lambda i, j: (i, 0)),
            pl.BlockSpec((1, D), lambda i, j: (0, 0)),
            pl.BlockSpec((D, tn), lambda i, j: (0, j)),
            pl.BlockSpec((1, tn), lambda i, j: (0, j)),
            pl.BlockSpec((D, LANES), lambda i, j: (0, 0)),
            pl.BlockSpec((1, LANES), lambda i, j: (0, 0)),
        ],
        out_specs=[
            pl.BlockSpec((tm, tn), lambda i, j: (i, j)),
            pl.BlockSpec((tm, LANES), lambda i, j: (i, 0)),
        ],
        out_shape=[jax.ShapeDtypeStruct((T, N), BF16), jax.ShapeDtypeStruct((T, LANES), F32)],
        scratch_shapes=[pltpu.VMEM((tm, D), BF16)],
        compiler_params=_cparams("parallel", "arbitrary"),
        name="in_proj",
    )(x, g, w, b, wf, bf)


def _conv_kernel(a_ref, b_ref, buf0_ref, w_ref, bdw_ref, lng_ref, lnb_ref, h_ref, nc_ref, u_ref, acc_ref,
                 *, tt, C, K):
    ti = pl.program_id(1)

    @pl.when(ti == 0)
    def _():
        u_ref[0:CONV_HALO, :] = buf0_ref[0]

    @pl.when(ti > 0)
    def _():
        u_ref[0:CONV_HALO, :] = u_ref[tt:tt + CONV_HALO, :]

    u_ref[CONV_HALO:CONV_HALO + tt, :] = a_ref[...].astype(F32) * jax.nn.sigmoid(b_ref[...].astype(F32))
    base = CONV_HALO - (K - 1)
    rc = min(tt, 128)
    for r0 in range(0, tt, rc):
        for c0 in range(0, C, LANES):
            acc = jnp.zeros((rc, LANES), F32)
            for j in range(K):
                acc = acc + w_ref[j:j + 1, c0:c0 + LANES] * u_ref[r0 + base + j:r0 + base + j + rc, c0:c0 + LANES]
            acc_ref[r0:r0 + rc, c0:c0 + LANES] = acc
    h = acc_ref[...] + bdw_ref[...]
    mu = jnp.mean(h, axis=-1, keepdims=True)
    d = h - mu
    var = jnp.mean(d * d, axis=-1, keepdims=True)
    y = d * lax.rsqrt(var + EPS) * lng_ref[...] + lnb_ref[...]
    h_ref[...] = (y * jax.nn.sigmoid(y)).astype(h_ref.dtype)

    @pl.when(ti == pl.num_programs(1) - 1)
    def _():
        nc_ref[0] = u_ref[tt:tt + CONV_HALO, :]


def _conv_branch(z, col0, B, L, buf0, w_dw, b_dw, ln_g, ln_b):
    K, C = w_dw.shape
    assert K - 1 <= CONV_HALO and L >= CONV_HALO and C % LANES == 0 and col0 % C == 0
    tt = _tile(L, 256)
    nt = L // tt
    cb = col0 // C
    wpad = jnp.zeros((CONV_HALO, C), F32).at[:K].set(w_dw)
    kern = functools.partial(_conv_kernel, tt=tt, C=C, K=K)
    return pl.pallas_call(
        kern,
        grid=(B, nt),
        in_specs=[
            pl.BlockSpec((tt, C), lambda b, t: (b * nt + t, cb)),
            pl.BlockSpec((tt, C), lambda b, t: (b * nt + t, cb + 1)),
            pl.BlockSpec((1, CONV_HALO, C), lambda b, t: (b, 0, 0)),
            _const_spec((CONV_HALO, C)),
            _const_spec((1, C)), _const_spec((1, C)), _const_spec((1, C)),
        ],
        out_specs=[
            pl.BlockSpec((tt, C), lambda b, t: (b * nt + t, 0)),
            pl.BlockSpec((1, CONV_HALO, C), lambda b, t: (b, 0, 0)),
        ],
        out_shape=[jax.ShapeDtypeStruct((B * L, C), BF16), jax.ShapeDtypeStruct((B, CONV_HALO, C), F32)],
        scratch_shapes=[pltpu.VMEM((CONV_HALO + tt, C), F32), pltpu.VMEM((tt, C), F32)],
        compiler_params=_cparams("parallel", "arbitrary"),
        name="conv_branch",
    )(z, z, buf0, wpad, b_dw[None], ln_g[None], ln_b[None])


def _ssm_matrices(a_re, a_im, log_dt, b_re, b_im, c_re, c_im):
    hp = lax.Precision.HIGHEST
    G, P, gw = b_re.shape
    S = SSM_CHUNK
    lam_re = jnp.minimum(a_re, -1e-4)
    lam_im = a_im
    dt = jnp.exp(log_dt)[:, None]
    d = jnp.arange(S + 1, dtype=F32)[None, :, None]
    mag = jnp.exp(lam_re[:, None, :] * dt[:, None, :] * d)
    ang = lam_im[:, None, :] * dt[:, None, :] * d
    pw_re, pw_im = mag * jnp.cos(ang), mag * jnp.sin(ang)
    den = lam_re * lam_re + lam_im * lam_im
    xr, xi = pw_re[:, 1] - 1.0, pw_im[:, 1]
    cf_re = (xr * lam_re + xi * lam_im) / den
    cf_im = (xi * lam_re - xr * lam_im) / den
    bb_re = cf_re[..., None] * b_re - cf_im[..., None] * b_im
    bb_im = cf_re[..., None] * b_im + cf_im[..., None] * b_re
    cp_re = c_re[:, None] * pw_re[:, :, None, :] - c_im[:, None] * pw_im[:, :, None, :]
    cp_im = c_re[:, None] * pw_im[:, :, None, :] + c_im[:, None] * pw_re[:, :, None, :]
    kk = (jnp.einsum('gdcp,gpe->gdce', cp_re[:, :S], bb_re, precision=hp)
          - jnp.einsum('gdcp,gpe->gdce', cp_im[:, :S], bb_im, precision=hp))
    s_idx = jnp.arange(S)[:, None]
    t_idx = jnp.arange(S)[None, :]
    lag = jnp.clip(t_idx - s_idx, 0, S - 1)
    m = kk[:, lag] * (t_idx >= s_idx)[None, :, :, None, None]
    m = m.transpose(0, 1, 4, 2, 3).reshape(G, S * gw, S * gw)
    rev_re = pw_re[:, S - 1 - jnp.arange(S)]
    rev_im = pw_im[:, S - 1 - jnp.arange(S)]
    p_re = rev_re[:, :, None, :] * bb_re.transpose(0, 2, 1)[:, None] - rev_im[:, :, None, :] * bb_im.transpose(0, 2, 1)[:, None]
    p_im = rev_re[:, :, None, :] * bb_im.transpose(0, 2, 1)[:, None] + rev_im[:, :, None, :] * bb_re.transpose(0, 2, 1)[:, None]
    p_re = p_re.reshape(G, S * gw, P)
    p_im = p_im.reshape(G, S * gw, P)
    w1 = jnp.concatenate([m, p_re, p_im, p_im, p_re], axis=-1)
    q_re = cp_re[:, 1:].transpose(0, 3, 1, 2).reshape(G, P, S * gw)
    q_im = -cp_im[:, 1:].transpose(0, 3, 1, 2).reshape(G, P, S * gw)
    q = jnp.concatenate([q_re, q_im], axis=1)
    ar, ai = pw_re[:, S], pw_im[:, S]
    a = jnp.zeros((G, 8, 2 * P), F32)
    a = a.at[:, 0].set(jnp.concatenate([ar, ar], -1))
    a = a.at[:, 1].set(jnp.concatenate([-ai, ai], -1))
    a = a.at[:, 2].set(jnp.concatenate([ai, -ai], -1))
    return w1.astype(BF16), q.astype(BF16), a


def _ssm_kernel(u_ref, w1_ref, q_ref, a_ref, h0_ref, h0s_ref, y_ref, hf_ref, tmp_ref, hh_ref, *, nk, B, CW, P2):
    tmp_ref[...] = jnp.dot(u_ref[0], w1_ref[0], preferred_element_type=F32)
    a1 = a_ref[0, 0:1, :]
    a2 = a_ref[0, 1:2, :]
    a2s = a_ref[0, 2:3, :]

    def body(k, carry):
        h, hs = carry
        r = pl.multiple_of(k * B, B)
        hh_ref[pl.ds(r, B), :] = h
        inc = tmp_ref[pl.ds(r, B), CW:CW + P2]
        incs = tmp_ref[pl.ds(r, B), CW + P2:CW + 2 * P2]
        return a1 * h + a2 * hs + inc, a1 * hs + a2s * h + incs

    h, _ = lax.fori_loop(0, nk, body, (h0_ref[0], h0s_ref[0]))
    hf_ref[0] = h
    y = tmp_ref[:, 0:CW] + jnp.dot(hh_ref[...].astype(BF16), q_ref[0], preferred_element_type=F32)
    y_ref[0] = y.astype(y_ref.dtype)


def _ssm_branch(u, B, L, mats, h0):
    w1, q, a = mats
    G, CW, _ = w1.shape
    P2 = q.shape[1]
    gw = CW // SSM_CHUNK
    assert CW == MXU_DIM and L % SSM_CHUNK == 0
    nk = L // SSM_CHUNK
    R = nk * B
    ug = u.reshape(B, nk, SSM_CHUNK, G, gw).transpose(3, 1, 0, 2, 4).reshape(G, R, CW)
    h0r = jnp.concatenate([h0[..., 0], h0[..., 1]], -1).transpose(1, 0, 2)
    h0s = jnp.concatenate([h0[..., 1], h0[..., 0]], -1).transpose(1, 0, 2)
    kern = functools.partial(_ssm_kernel, nk=nk, B=B, CW=CW, P2=P2)
    y, hf = pl.pallas_call(
        kern,
        grid=(G,),
        in_specs=[
            pl.BlockSpec((1, R, CW), lambda g: (g, 0, 0)),
            pl.BlockSpec((1, CW, CW + 2 * P2), lambda g: (g, 0, 0)),
            pl.BlockSpec((1, P2, CW), lambda g: (g, 0, 0)),
            pl.BlockSpec((1, 8, P2), lambda g: (g, 0, 0)),
            pl.BlockSpec((1, B, P2), lambda g: (g, 0, 0)),
            pl.BlockSpec((1, B, P2), lambda g: (g, 0, 0)),
        ],
        out_specs=[
            pl.BlockSpec((1, R, CW), lambda g: (g, 0, 0)),
            pl.BlockSpec((1, B, P2), lambda g: (g, 0, 0)),
        ],
        out_shape=[jax.ShapeDtypeStruct((G, R, CW), BF16), jax.ShapeDtypeStruct((G, B, P2), F32)],
        scratch_shapes=[pltpu.VMEM((R, CW + 2 * P2), F32), pltpu.VMEM((R, P2), F32)],
        compiler_params=_cparams("parallel"),
        name="ssm_branch",
    )(ug, w1, q, a, h0r, h0s)
    y = y.reshape(G, nk, B, SSM_CHUNK, gw).transpose(2, 1, 3, 0, 4).reshape(B * L, G * gw)
    P = P2 // 2
    new_state = jnp.stack([hf[..., :P], hf[..., P:]], axis=-1).transpose(1, 0, 2, 3)
    return y, new_state


def _bias_table(H):
    assert H % 2 == 0 and BIAS_SPLIT * H <= LANES
    pk = np.zeros((LANES, (H // 2) * LANES), np.float32)
    for h in range(H):
        p, e = divmod(h, 2)
        for i in range(BIAS_SPLIT):
            pk[i * H + h, p * LANES + e * BIAS_SPLIT + i] = -1.0
    return jnp.asarray(pk, BF16)


def _stack_pair_queries(q, e, hd):
    lane = lax.broadcasted_iota(jnp.int32, (1, LANES), 1)
    qe = jnp.where((lane >= e * hd) & (lane < (e + 1) * hd), q, jnp.zeros_like(q))
    one = jnp.where((lane >= e * BIAS_SPLIT) & (lane < (e + 1) * BIAS_SPLIT), 1.0, 0.0).astype(q.dtype)
    return jnp.concatenate([qe, jnp.broadcast_to(one, q.shape)], axis=1)


def _split3(x):
    hi = x.astype(BF16)
    r1 = x - hi.astype(F32)
    mid = r1.astype(BF16)
    lo = (r1 - mid.astype(F32)).astype(BF16)
    return hi, mid, lo


def _forget_cumsum(lf, carry, H):
    tl = lf.shape[0]
    lane = lax.broadcasted_iota(jnp.int32, (1, LANES), 1)
    lf = jnp.where(lane < H, lf, 0.0)
    tri = (lax.broadcasted_iota(jnp.int32, (tl, tl), 0) >= lax.broadcasted_iota(jnp.int32, (tl, tl), 1)).astype(BF16)
    hi, mid, lo = _split3(lf)
    c = (jnp.dot(tri, hi, preferred_element_type=F32) + jnp.dot(tri, mid, preferred_element_type=F32)
         + jnp.dot(tri, lo, preferred_element_type=F32))
    return c + carry


def _pack_split(c, H):
    hi, mid, lo = _split3(c)
    packed = hi.astype(F32) + pltpu.roll(mid.astype(F32), H, axis=1) + pltpu.roll(lo.astype(F32), 2 * H, axis=1)
    return packed.astype(BF16)


def _pair_rmsnorm(x, gain, lo_half, hd):
    sq = x * x
    s_lo = jnp.sum(jnp.where(lo_half, sq, 0.0), axis=-1, keepdims=True)
    s_hi = jnp.sum(jnp.where(lo_half, 0.0, sq), axis=-1, keepdims=True)
    ms = jnp.where(lo_half, s_lo, s_hi) * (1.0 / hd)
    return x * lax.rsqrt(ms + EPS) * gain


def _qk_kernel(*refs, tl, H, hd, scale, aliased, time_minor, own):
    zq_ref, zk_ref, zv_ref, lf_ref, qg_ref, kg_ref, pk_ref = refs[:7]
    qa_ref, ka_ref, k5_ref, v5_ref, c_ref = refs[9:] if aliased else refs[7:]

    @pl.when(pl.program_id(1) == 0)
    def _():
        c_ref[...] = jnp.zeros_like(c_ref)

    for other in range(k5_ref.shape[0]):
        if other != own:
            k5_ref[other] = jnp.zeros(k5_ref.shape[1:], F32)
            v5_ref[other] = jnp.zeros(v5_ref.shape[1:], F32)

    c = _forget_cumsum(lf_ref[...], c_ref[0:1, :], H)
    c_ref[...] = jnp.broadcast_to(c[tl - 1:tl, :], c_ref.shape)
    bk = jnp.dot(_pack_split(c, H), pk_ref[...], preferred_element_type=F32)
    lo_half = lax.broadcasted_iota(jnp.int32, (1, LANES), 1) < hd
    for p in range(H // 2):
        sl = slice(p * LANES, (p + 1) * LANES)
        qn = _pair_rmsnorm(zq_ref[:, sl].astype(F32), qg_ref[:, sl], lo_half, hd) * scale
        kn = _pair_rmsnorm(zk_ref[:, sl].astype(F32), kg_ref[:, sl], lo_half, hd)
        vv = zv_ref[:, sl].astype(F32)
        qa_ref[:, sl] = qn.astype(BF16)
        ka_ref[:, 2 * p * LANES:(2 * p + 1) * LANES] = kn.astype(BF16)
        ka_ref[:, (2 * p + 1) * LANES:(2 * p + 2) * LANES] = bk[:, sl].astype(BF16)
        if time_minor:
            k5_ref[own, 0, sl, :] = kn.T
            v5_ref[own, 0, sl, :] = vv.T
        else:
            for e in range(2):
                k5_ref[own, 0, :, 2 * p + e, :] = kn[:, e * hd:(e + 1) * hd]
                v5_ref[own, 0, :, 2 * p + e, :] = vv[:, e * hd:(e + 1) * hd]


def _qk_prep(z, qcol, B, L, logf, q_gain, k_gain, H, layer, depth, kv_prev):
    hd = q_gain.shape[0]
    AW = H * hd
    assert 2 * hd == LANES and qcol % AW == 0
    tl = _tile(L, 256)
    nt = L // tl
    qb = qcol // AW
    pk = _bias_table(H)
    NA = H * LANES
    aliased = kv_prev is not None
    time_minor = tl % LANES == 0
    nl, l0 = (1, layer) if aliased else (depth, 0)
    kern = functools.partial(_qk_kernel, tl=tl, H=H, hd=hd, scale=hd ** -0.5, aliased=aliased,
                             time_minor=time_minor, own=layer - l0)
    row = lambda b, t: b * nt + t
    in_specs = [
        pl.BlockSpec((tl, AW), lambda b, t: (row(b, t), qb)),
        pl.BlockSpec((tl, AW), lambda b, t: (row(b, t), qb + 1)),
        pl.BlockSpec((tl, AW), lambda b, t: (row(b, t), qb + 2)),
        pl.BlockSpec((tl, LANES), lambda b, t: (row(b, t), 0)),
        _const_spec((1, AW)), _const_spec((1, AW)), _const_spec(pk.shape),
    ]
    args = [z, z, z, logf, jnp.tile(q_gain, H)[None], jnp.tile(k_gain, H)[None], pk]
    if aliased:
        in_specs += [pl.BlockSpec(memory_space=pl.ANY), pl.BlockSpec(memory_space=pl.ANY)]
        args += list(kv_prev)
    if time_minor:
        kv_spec = pl.BlockSpec((nl, 1, AW, tl), lambda b, t: (l0, b, 0, t))
        kv_shape = jax.ShapeDtypeStruct((depth, B, AW, L), F32)
    else:
        kv_spec = pl.BlockSpec((nl, 1, tl, H, hd), lambda b, t: (l0, b, t, 0, 0))
        kv_shape = jax.ShapeDtypeStruct((depth, B, L, H, hd), F32)
    return pl.pallas_call(
        kern,
        grid=(B, nt),
        in_specs=in_specs,
        out_specs=[
            pl.BlockSpec((tl, AW), lambda b, t: (row(b, t), 0)),
            pl.BlockSpec((tl, NA), lambda b, t: (row(b, t), 0)),
            kv_spec, kv_spec,
        ],
        out_shape=[jax.ShapeDtypeStruct((B * L, AW), BF16), jax.ShapeDtypeStruct((B * L, NA), BF16),
                   kv_shape, kv_shape],
        scratch_shapes=[pltpu.VMEM((8, LANES), F32)],
        input_output_aliases={7: 2, 8: 3} if aliased else {},
        compiler_params=_cparams("parallel", "arbitrary"),
        name="qk_prep",
    )(*args)


def _attn_kernel(q_ref, k_ref, v_ref, o_ref, q2_ref, acc_ref, m_ref, *, tq, tk, hd):
    qi = pl.program_id(2)
    q = q_ref[...]
    for e in range(2):
        q2_ref[e * tq:(e + 1) * tq, :] = _stack_pair_queries(q, e, hd)
    q_lo = qi * tq
    n_full = (q_lo + 1) // tk
    n_tot = (q_lo + tq + tk - 1) // tk
    m_ref[...] = jnp.full(m_ref.shape, NEG, F32)
    acc_ref[...] = jnp.zeros(acc_ref.shape, F32)
    ones = jnp.ones((tk, LANES), BF16)

    def step(j, masked):
        k0 = pl.multiple_of(j * tk, tk)
        mask = None
        if masked:
            row = lax.broadcasted_iota(jnp.int32, (2 * tq, 1), 0)
            qpos = q_lo + jnp.where(row >= tq, row - tq, row)
            mask = k0 + lax.broadcasted_iota(jnp.int32, (1, tk), 1) <= qpos
        vs = jnp.concatenate([v_ref[pl.ds(k0, tk), :], ones], axis=1)
        _softmax_step(q2_ref[...], k_ref[pl.ds(k0, tk), :], vs, mask, m_ref, acc_ref)

    def full_body(j, c):
        step(j, False)
        return c

    def edge_body(j, c):
        step(j, True)
        return c

    lax.fori_loop(0, n_full, full_body, 0)
    lax.fori_loop(n_full, n_tot, edge_body, 0)
    o_ref[...] = _pair_output(acc_ref[...], tq, hd).astype(o_ref.dtype)


def _softmax_step(q2, k_aug, v_aug, mask, m_ref, acc_ref):
    tk = k_aug.shape[0]
    s = lax.dot_general(q2, k_aug, (((1,), (1,)), ((), ())), preferred_element_type=F32)
    if mask is not None:
        s = jnp.where(mask, s, NEG)
    m_prev = m_ref[...]
    m_next = jnp.maximum(m_prev, jnp.max(s, axis=-1, keepdims=True))
    alpha = jnp.exp(m_prev - m_next)
    p = jnp.exp(s - jnp.concatenate([m_next] * (tk // LANES), axis=1))
    pv = jnp.dot(p.astype(BF16), v_aug, preferred_element_type=F32)
    acc_ref[...] = jnp.concatenate([alpha, alpha], axis=1) * acc_ref[...] + pv
    m_ref[...] = m_next


def _pair_output(acc, tq, hd):
    o = acc[:, 0:LANES] / acc[:, LANES:2 * LANES]
    lane = lax.broadcasted_iota(jnp.int32, (1, LANES), 1)
    return jnp.where(lane < hd, o[0:tq], o[tq:2 * tq])


def _attention(qa, ka, v, vcol, B, L, H, hd, tq, tk):
    npair = H // 2
    nq = L // tq
    assert L % tq == 0 and L % tk == 0 and tk % LANES == 0 and vcol % LANES == 0
    vb = vcol // LANES
    kern = functools.partial(_attn_kernel, tq=tq, tk=tk, hd=hd)
    return pl.pallas_call(
        kern,
        grid=(B, npair, nq),
        in_specs=[
            pl.BlockSpec((tq, LANES), lambda b, p, i: (b * nq + i, p)),
            pl.BlockSpec((L, 2 * LANES), lambda b, p, i: (b, p)),
            pl.BlockSpec((L, LANES), lambda b, p, i: (b, vb + p)),
        ],
        out_specs=pl.BlockSpec((tq, LANES), lambda b, p, i: (b * nq + i, p)),
        out_shape=jax.ShapeDtypeStruct((B * L, H * hd), BF16),
        scratch_shapes=[pltpu.VMEM((2 * tq, 2 * LANES), BF16), pltpu.VMEM((2 * tq, 2 * LANES), F32),
                        pltpu.VMEM((2 * tq, LANES), F32)],
        compiler_params=_cparams("parallel", "parallel", "arbitrary"),
        name="fox_attention",
    )(qa, ka, v)


def _cached_attn_kernel(q_ref, kc_ref, vc_ref, lf_ref, kn_ref, vn_ref, o_ref,
                        q2_ref, acc_ref, m_ref, l_ref, c_ref, *, Lq, tl, H, hd):
    t = pl.program_id(1)
    npair = H // 2
    row = lax.broadcasted_iota(jnp.int32, (2 * Lq, 1), 0)

    @pl.when(t == 0)
    def _():
        c_ref[...] = jnp.zeros_like(c_ref)
        m_ref[...] = jnp.full(m_ref.shape, NEG, F32)
        l_ref[...] = jnp.zeros(l_ref.shape, F32)
        acc_ref[...] = jnp.zeros(acc_ref.shape, F32)
        for p in range(npair):
            q = q_ref[:, p * LANES:(p + 1) * LANES]
            for e in range(2):
                q2_ref[p, e * Lq:(e + 1) * Lq, :] = _stack_pair_queries(q, e, hd)

    def update(p, s, weigh):
        m_prev = m_ref[p]
        m_next = jnp.maximum(m_prev, jnp.max(s, axis=-1, keepdims=True))
        alpha = jnp.exp(m_prev - m_next)
        pr = jnp.exp(s - jnp.concatenate([m_next] * (s.shape[1] // LANES), axis=1))
        l_ref[p] = alpha * l_ref[p] + jnp.sum(pr, axis=-1, keepdims=True)
        acc_ref[p] = alpha * acc_ref[p] + weigh(pr.astype(BF16))
        m_ref[p] = m_next

    lf = lf_ref[0, 0]
    after = (lax.broadcasted_iota(jnp.int32, (tl, tl), 0) > lax.broadcasted_iota(jnp.int32, (tl, tl), 1)).astype(BF16)
    later = sum(jnp.dot(piece, after, preferred_element_type=F32) for piece in _split3(lf))
    later = later + jnp.concatenate([c_ref[...]] * (tl // LANES), axis=1)
    c_ref[...] = c_ref[...] + jnp.sum(lf, axis=-1, keepdims=True)
    nt_dims = (((1,), (1,)), ((), ()))
    for p in range(npair):
        sl = slice(p * LANES, (p + 1) * LANES)
        kp = kc_ref[0, 0, sl, :].astype(BF16)
        vp = vc_ref[0, 0, sl, :].astype(BF16)
        s = jnp.dot(q2_ref[p, :, 0:LANES], kp, preferred_element_type=F32)
        s = s + jnp.where(row < Lq, later[2 * p:2 * p + 1, :], later[2 * p + 1:2 * p + 2, :])
        update(p, s, lambda pr: lax.dot_general(pr, vp, nt_dims, preferred_element_type=F32))

    @pl.when(t == pl.num_programs(1) - 1)
    def _():
        nk = kn_ref.shape[0]
        mask = lax.broadcasted_iota(jnp.int32, (1, nk), 1) <= jnp.where(row >= Lq, row - Lq, row)
        lane = lax.broadcasted_iota(jnp.int32, (1, LANES), 1)
        for p in range(npair):
            s = lax.dot_general(q2_ref[p], kn_ref[:, 2 * p * LANES:(2 * p + 2) * LANES], nt_dims,
                                preferred_element_type=F32)
            vn = vn_ref[:, p * LANES:(p + 1) * LANES]
            update(p, jnp.where(mask, s, NEG), lambda pr: jnp.dot(pr, vn, preferred_element_type=F32))
            o = acc_ref[p] / l_ref[p]
            o_ref[:, p * LANES:(p + 1) * LANES] = jnp.where(lane < hd, o[0:Lq], o[Lq:2 * Lq]).astype(o_ref.dtype)


def _cached_attention(qa, ka_new, v_new, cache_kt, cache_vt, cache_lft, layer, B, Lq, H, hd):
    PL = cache_kt.shape[-1]
    AW = H * hd
    NA = H * LANES
    nk = ka_new.shape[0] // B
    tl = _tile(PL, 1024, LANES)
    nt = PL // tl
    npair = H // 2
    kern = functools.partial(_cached_attn_kernel, Lq=Lq, tl=tl, H=H, hd=hd)
    kv_spec = pl.BlockSpec((1, 1, AW, tl), lambda b, t: (layer, b, 0, nt - 1 - t))
    return pl.pallas_call(
        kern,
        grid=(B, nt),
        in_specs=[
            pl.BlockSpec((Lq, AW), lambda b, t: (b, 0)),
            kv_spec, kv_spec,
            pl.BlockSpec((1, 1, H, tl), lambda b, t: (layer, b, 0, nt - 1 - t)),
            pl.BlockSpec((nk, NA), lambda b, t: (b, 0)),
            pl.BlockSpec((nk, AW), lambda b, t: (b, 0)),
        ],
        out_specs=pl.BlockSpec((Lq, AW), lambda b, t: (b, 0)),
        out_shape=jax.ShapeDtypeStruct((B * Lq, AW), BF16),
        scratch_shapes=[pltpu.VMEM((npair, 2 * Lq, 2 * LANES), BF16), pltpu.VMEM((npair, 2 * Lq, LANES), F32),
                        pltpu.VMEM((npair, 2 * Lq, LANES), F32), pltpu.VMEM((npair, 2 * Lq, LANES), F32),
                        pltpu.VMEM((H, LANES), F32)],
        compiler_params=_cparams("parallel", "arbitrary"),
        name="fox_cached_attention",
    )(qa, cache_kt, cache_vt, cache_lft, ka_new, v_new)


def _merge_kernel(x_ref, ha_ref, ys_ref, u_ref, o_ref, ga_ref, gb_ref, gc_ref, d_ref, wglu_ref, bglu_ref,
                  wa_ref, wb_ref, wc_ref, wo_ref, out_ref):
    ys = ys_ref[...].astype(F32) + d_ref[...] * u_ref[...].astype(F32)
    hs = 0.5 * ys * (1.0 + jnp.tanh(math.sqrt(2.0 / math.pi) * (ys + 0.044715 * (ys * ys * ys))))
    gl = jnp.dot(hs.astype(BF16), wglu_ref[...], preferred_element_type=F32) + bglu_ref[...]
    hs = hs * jax.nn.sigmoid(gl)
    oa = jnp.dot(ha_ref[...], wa_ref[...], preferred_element_type=F32)
    ob = jnp.dot(hs.astype(BF16), wb_ref[...], preferred_element_type=F32)
    oc = jnp.dot(o_ref[...], wc_ref[...], preferred_element_type=F32)
    m = (jax.nn.sigmoid(ga_ref[...].astype(F32)) * oa + jax.nn.sigmoid(gb_ref[...].astype(F32)) * ob
         + jax.nn.sigmoid(gc_ref[...].astype(F32)) * oc)
    out_ref[...] = x_ref[...] + jnp.dot(m.astype(BF16), wo_ref[...], preferred_element_type=F32)


def _merge(x, ha, ys, z, ucol, gcol, o, d, wglu, bglu, wa, wb, wc, wo):
    T, D = x.shape
    C, S, AW = ha.shape[1], ys.shape[1], o.shape[1]
    assert ucol % S == 0 and gcol % D == 0
    tm = _tile(T, 256)
    ub, gb = ucol // S, gcol // D
    single = pl.Buffered(1)
    wspec = lambda a: pl.BlockSpec(a.shape, lambda i: (0, 0), pipeline_mode=single)
    return pl.pallas_call(
        _merge_kernel,
        grid=(T // tm,),
        in_specs=[
            pl.BlockSpec((tm, D), lambda i: (i, 0)),
            pl.BlockSpec((tm, C), lambda i: (i, 0)),
            pl.BlockSpec((tm, S), lambda i: (i, 0)),
            pl.BlockSpec((tm, S), lambda i: (i, ub)),
            pl.BlockSpec((tm, AW), lambda i: (i, 0)),
            pl.BlockSpec((tm, D), lambda i: (i, gb)),
            pl.BlockSpec((tm, D), lambda i: (i, gb + 1)),
            pl.BlockSpec((tm, D), lambda i: (i, gb + 2)),
            _const_spec((1, S)), wspec(wglu), _const_spec((1, S)),
            wspec(wa), wspec(wb), wspec(wc), wspec(wo),
        ],
        out_specs=pl.BlockSpec((tm, D), lambda i: (i, 0)),
        out_shape=jax.ShapeDtypeStruct((T, D), F32),
        compiler_params=_cparams("parallel"),
        name="merge",
    )(x, ha, ys, z, o, z, z, z, d[None], wglu, bglu[None], wa, wb, wc, wo)


def _route_kernel(x_ref, g_ref, wh_ref, wl_ref, b_ref, r_ref, *, ng, epg):
    x = x_ref[...]
    xn = x * lax.rsqrt(jnp.mean(x * x, axis=-1, keepdims=True) + EPS) * g_ref[...]
    xh = xn.astype(BF16)
    xl = (xn - xh.astype(F32)).astype(BF16)
    wh = wh_ref[...]
    lg = (jnp.dot(xh, wh, preferred_element_type=F32) + jnp.dot(xl, wh, preferred_element_type=F32)
          + jnp.dot(xh, wl_ref[...], preferred_element_type=F32) + b_ref[...])
    lane = lax.broadcasted_iota(jnp.int32, (1, LANES), 1).astype(F32)
    big = float(LANES)
    is_g = lane < ng
    gmax = jnp.max(jnp.where(is_g, lg, NEG), axis=-1, keepdims=True)
    den = jnp.sum(jnp.where(is_g, jnp.exp(jnp.where(is_g, lg, NEG) - gmax), 0.0), axis=-1, keepdims=True)
    p_top = 1.0 / den
    g_top = jnp.min(jnp.where(is_g & (lg == gmax), lane, big), axis=-1, keepdims=True)
    sel = (lane >= ng + g_top * epg) & (lane < ng + (g_top + 1.0) * epg)
    le = jnp.where(sel, lg, NEG)
    v1 = jnp.max(le, axis=-1, keepdims=True)
    i1 = jnp.min(jnp.where(sel & (le == v1), lane, big), axis=-1, keepdims=True)
    sel2 = sel & (lane != i1)
    le2 = jnp.where(sel2, lg, NEG)
    v2 = jnp.max(le2, axis=-1, keepdims=True)
    i2 = jnp.min(jnp.where(sel2 & (le2 == v2), lane, big), axis=-1, keepdims=True)
    t = jnp.exp(v2 - v1)
    w1 = p_top / (1.0 + t)
    w2 = p_top * t / (1.0 + t)
    r_ref[...] = jnp.where(lane == 0, i1 - ng, jnp.where(lane == 1, i2 - ng, jnp.where(lane == 2, w1, jnp.where(lane == 3, w2, 0.0))))


def _route(x, g, w_route_g, b_route_g, w_route_e, b_route_e):
    T, D = x.shape
    ng, ne = w_route_g.shape[1], w_route_e.shape[1]
    assert ng + ne <= LANES
    w = jnp.zeros((D, LANES), F32).at[:, :ng].set(w_route_g).at[:, ng:ng + ne].set(w_route_e)
    b = jnp.zeros((1, LANES), F32).at[0, :ng].set(b_route_g).at[0, ng:ng + ne].set(b_route_e)
    wh = w.astype(BF16)
    wl = (w - wh.astype(F32)).astype(BF16)
    tm = _tile(T, 512)
    return pl.pallas_call(
        functools.partial(_route_kernel, ng=ng, epg=ne // ng),
        grid=(T // tm,),
        in_specs=[pl.BlockSpec((tm, D), lambda i: (i, 0)), _const_spec((1, D)),
                  _const_spec((D, LANES)), _const_spec((D, LANES)), _const_spec((1, LANES))],
        out_specs=pl.BlockSpec((tm, LANES), lambda i: (i, 0)),
        out_shape=jax.ShapeDtypeStruct((T, LANES), F32),
        compiler_params=_cparams("parallel"),
        name="route",
    )(x, g[None], wh, wl, b)


def _dispatch_plan(route, n_exp, tmo):
    T = route.shape[0]
    eid = route[:, 0:2].astype(jnp.int32).reshape(-1)
    wgt = route[:, 2:4].reshape(-1)
    onehot = (eid[:, None] == jnp.arange(n_exp)[None, :]).astype(jnp.int32)
    csum = jnp.cumsum(onehot, axis=0)
    rank = jnp.sum((csum - onehot) * onehot, axis=1)
    counts = csum[-1]
    padded = ((counts + tmo - 1) // tmo) * tmo
    ends = jnp.cumsum(padded)
    starts = ends - padded
    pos = starts[eid] + rank
    NP = 2 * T + n_exp * tmo
    tok = jnp.zeros((NP,), jnp.int32).at[pos].set(jnp.arange(2 * T, dtype=jnp.int32) // 2)
    wrow = jnp.zeros((NP,), F32).at[pos].set(wgt)
    tile_start = jnp.arange(NP // tmo, dtype=jnp.int32) * tmo
    tile_exp = jnp.minimum(jnp.sum((tile_start[:, None] >= ends[None, :]).astype(jnp.int32), axis=1), n_exp - 1)
    n_used = (ends[-1] // tmo).astype(jnp.int32).reshape(1)
    return tok, wrow[:, None], tile_exp, n_used, pos.astype(jnp.int32)


def _moe_kernel(te_ref, tok_ref, nu_ref, x_hbm, g_ref, w_ref, wg_ref, wu_ref, wd_ref, y_ref, xbuf, sem, *, tmo):
    i = pl.program_id(0)
    n_used = nu_ref[0]
    slot = i % 2

    def row_copy(tile, dst_slot, r):
        t = tok_ref[tile * tmo + r]
        return pltpu.make_async_copy(x_hbm.at[pl.ds(t, 1), :], xbuf.at[dst_slot, pl.ds(r, 1), :], sem.at[dst_slot])

    def experts():
        pltpu.make_async_copy(x_hbm.at[pl.ds(0, tmo), :], xbuf.at[slot], sem.at[slot]).wait()
        x = xbuf[slot]
        xn = (x * lax.rsqrt(jnp.mean(x * x, axis=-1, keepdims=True) + EPS) * g_ref[...]).astype(BF16)
        hg = jnp.dot(xn, wg_ref[0], preferred_element_type=F32)
        hu = jnp.dot(xn, wu_ref[0], preferred_element_type=F32)
        h = (hg * jax.nn.sigmoid(hg) * hu).astype(BF16)
        y_ref[...] = jnp.dot(h, wd_ref[0], preferred_element_type=F32) * w_ref[...]

    @pl.when(i == 0)
    def _():
        def issue(r, c):
            row_copy(0, 0, r).start()
            return c
        lax.fori_loop(0, tmo, issue, 0)

    @pl.when(i + 1 < n_used)
    def _():
        for r in range(tmo):
            row_copy(i + 1, 1 - slot, r).start()
        experts()

    @pl.when(i + 1 == n_used)
    def _():
        experts()

    @pl.when(i >= n_used)
    def _():
        y_ref[...] = jnp.zeros_like(y_ref)


def _moe(x, g, plan, wg, wu, wd, tmo):
    T, D = x.shape
    tok, wrow, tile_exp, n_used, _ = plan
    NP = tok.shape[0]
    E, _, DE = wg.shape
    grid_spec = pltpu.PrefetchScalarGridSpec(
        num_scalar_prefetch=3,
        grid=(NP // tmo,),
        in_specs=[
            pl.BlockSpec(memory_space=pl.ANY),
            pl.BlockSpec((1, D), lambda i, te, tk, nu: (0, 0)),
            pl.BlockSpec((tmo, 1), lambda i, te, tk, nu: (i, 0)),
            pl.BlockSpec((1, D, DE), lambda i, te, tk, nu: (te[i], 0, 0)),
            pl.BlockSpec((1, D, DE), lambda i, te, tk, nu: (te[i], 0, 0)),
            pl.BlockSpec((1, DE, D), lambda i, te, tk, nu: (te[i], 0, 0)),
        ],
        out_specs=pl.BlockSpec((tmo, D), lambda i, te, tk, nu: (i, 0)),
        scratch_shapes=[pltpu.VMEM((2, tmo, D), F32), pltpu.SemaphoreType.DMA((2,))],
    )
    return pl.pallas_call(
        functools.partial(_moe_kernel, tmo=tmo),
        grid_spec=grid_spec,
        out_shape=jax.ShapeDtypeStruct((NP, D), F32),
        compiler_params=_cparams("arbitrary"),
        name="moe_experts",
    )(tile_exp, tok, n_used, x, g[None], wrow, wg, wu, wd)


def _combine_kernel(pos_ref, x_ref, y_hbm, out_ref, ybuf, sem, *, tc):
    i = pl.program_id(0)
    n = pl.num_programs(0)

    slot = i % 2

    def row_copy(tile, dst_slot, r, s):
        p = pos_ref[2 * (tile * tc + r) + s]
        return pltpu.make_async_copy(y_hbm.at[pl.ds(p, 1), :], ybuf.at[dst_slot, s, pl.ds(r, 1), :], sem.at[dst_slot])

    @pl.when(i == 0)
    def _():
        def issue(r, c):
            for s in range(2):
                row_copy(0, 0, r, s).start()
            return c
        lax.fori_loop(0, tc, issue, 0)

    @pl.when(i + 1 < n)
    def _():
        for r in range(tc):
            for s in range(2):
                row_copy(i + 1, 1 - slot, r, s).start()

    for s in range(2):
        pltpu.make_async_copy(y_hbm.at[pl.ds(0, tc), :], ybuf.at[slot, s], sem.at[slot]).wait()
    out_ref[...] = x_ref[...] + (ybuf[slot, 0] + ybuf[slot, 1])


def _combine(x, y_sorted, pos):
    T, D = x.shape
    tc = _tile(T, 256)
    grid_spec = pltpu.PrefetchScalarGridSpec(
        num_scalar_prefetch=1,
        grid=(T // tc,),
        in_specs=[pl.BlockSpec((tc, D), lambda i, p: (i, 0)), pl.BlockSpec(memory_space=pl.ANY)],
        out_specs=pl.BlockSpec((tc, D), lambda i, p: (i, 0)),
        scratch_shapes=[pltpu.VMEM((2, 2, tc, D), F32), pltpu.SemaphoreType.DMA((2,))],
    )
    return pl.pallas_call(
        functools.partial(_combine_kernel, tc=tc),
        grid_spec=grid_spec,
        out_shape=jax.ShapeDtypeStruct((T, D), F32),
        compiler_params=_cparams("arbitrary"),
        name="moe_combine",
    )(pos, x, y_sorted)


def _layer_weights(p, H):
    D = p['w_in'].shape[0]
    C = p['w_dw'].shape[1]
    S = p['ssm_d'].shape[0]
    AW = H * p['q_gain'].shape[0]
    o_u = 2 * C
    o_q = o_u + S
    o_f = o_q + 3 * AW
    o_g = o_f + H
    perm = lambda a: jnp.concatenate([a[..., o_g:], a[..., o_q:o_f], a[..., :o_u], a[..., o_u:o_q]], axis=-1)
    cols = dict(g=0, q=3 * D, conv=3 * D + 3 * AW, u=3 * D + 3 * AW + 2 * C)
    wf = jnp.zeros((D, LANES), F32).at[:, :H].set(p['w_in'][:, o_f:o_g]).astype(BF16)
    bf = jnp.zeros((1, LANES), F32).at[0, :H].set(p['b_in'][o_f:o_g])
    return perm(p['w_in']).astype(BF16), perm(p['b_in'])[None], wf, bf, cols


def _mixer_and_moe(x3, p, conv_buf, ssm_h0, past, H, layer, depth, kv_prev):
    B, L, D = x3.shape
    T = B * L
    x = x3.reshape(T, D)
    hd = p['q_gain'].shape[0]
    AW = H * hd
    K, C = p['w_dw'].shape
    G, P, gw = p['ssm_b_re'].shape
    S = G * gw
    w_main, b_main, wf, bf, cols = _layer_weights(p, H)
    z, logf = _in_proj(x, p['g_mix'][None], w_main, b_main, wf, bf)

    buf0 = jnp.zeros((B, CONV_HALO, C), F32)
    if conv_buf is not None:
        buf0 = buf0.at[:, CONV_HALO - (K - 1):].set(conv_buf)
    ha, nc = _conv_branch(z, cols['conv'], B, L, buf0, p['w_dw'], p['b_dw'], p['ln_g'], p['ln_b'])
    new_conv = nc[:, CONV_HALO - (K - 1):]

    mats = _ssm_matrices(p['ssm_a_re'], p['ssm_a_im'], p['ssm_log_dt'], p['ssm_b_re'], p['ssm_b_im'],
                         p['ssm_c_re'], p['ssm_c_im'])
    h0 = jnp.zeros((B, G, P, 2), F32) if ssm_h0 is None else ssm_h0
    ys, new_ssm = _ssm_branch(z[:, cols['u']:cols['u'] + S], B, L, mats, h0)

    vcol = cols['q'] + 2 * AW
    qa, ka, k5, v5 = _qk_prep(z, cols['q'], B, L, logf, p['q_gain'], p['k_gain'], H, layer, depth, kv_prev)
    if past is None:
        o = _attention(qa, ka, z, vcol, B, L, H, hd, _tile(L, 256), _tile(L, 512, LANES))
    else:
        nk = ((L + LANES - 1) // LANES) * LANES
        pad_rows = lambda a: jnp.pad(a.reshape(B, L, -1), ((0, 0), (0, nk - L), (0, 0))).reshape(B * nk, -1)
        o = _cached_attention(qa, pad_rows(ka), pad_rows(z[:, vcol:vcol + AW]), *past, layer, B, L, H, hd)

    bfw = lambda a: a.astype(BF16)
    x = _merge(x, ha, ys, z, cols['u'], cols['g'], o, p['ssm_d'], bfw(p['w_glu']), p['b_glu'],
               bfw(p['w_conv_out']), bfw(p['w_ssm_out']), bfw(p['w_attn_out']), bfw(p['w_out']))

    route = _route(x, p['g_ffn'], p['w_route_g'], p['b_route_g'], p['w_route_e'], p['b_route_e'])
    E = p['w_e_gate'].shape[0]
    tmo = _tile(T, 256)
    plan = _dispatch_plan(route, E, tmo)
    y_sorted = _moe(x, p['g_ffn'], plan, bfw(p['w_e_gate']), bfw(p['w_e_up']), bfw(p['w_e_down']), tmo)
    x = _combine(x, y_sorted, plan[4])

    state = (new_conv, new_ssm, logf[:, :H].reshape(B, L, H))
    return x.reshape(B, L, D), state, (k5, v5)


def kernel(x_prompt, x_sample, cache_conv, state_ssm, cache_k, cache_v, cache_logf, g_mix, w_in, b_in, w_dw, b_dw, ln_g, ln_b, w_conv_out, ssm_a_re, ssm_a_im, ssm_log_dt, ssm_b_re, ssm_b_im, ssm_c_re, ssm_c_im, ssm_d, w_glu, b_glu, w_ssm_out, q_gain, k_gain, w_attn_out, w_out, g_ffn, w_route_g, b_route_g, w_route_e, b_route_e, w_e_gate, w_e_up, w_e_down):
    depth = w_in.shape[0]
    H = cache_logf.shape[-1]
    stacked = dict(g_mix=g_mix, w_in=w_in, b_in=b_in, w_dw=w_dw, b_dw=b_dw, ln_g=ln_g, ln_b=ln_b,
                   w_conv_out=w_conv_out, ssm_a_re=ssm_a_re, ssm_a_im=ssm_a_im, ssm_log_dt=ssm_log_dt,
                   ssm_b_re=ssm_b_re, ssm_b_im=ssm_b_im, ssm_c_re=ssm_c_re, ssm_c_im=ssm_c_im, ssm_d=ssm_d,
                   w_glu=w_glu, b_glu=b_glu, w_ssm_out=w_ssm_out, q_gain=q_gain, k_gain=k_gain,
                   w_attn_out=w_attn_out, w_out=w_out, g_ffn=g_ffn, w_route_g=w_route_g, b_route_g=b_route_g,
                   w_route_e=w_route_e, b_route_e=b_route_e, w_e_gate=w_e_gate, w_e_up=w_e_up, w_e_down=w_e_down)
    xp, xs = x_prompt, x_sample
    st_p, st_s = [], []
    kv_p = kv_s = None
    nb, PL, hd = cache_k.shape[1], cache_k.shape[2], cache_k.shape[4]
    past = (jnp.transpose(cache_k, (0, 1, 3, 4, 2)).reshape(depth, nb, H * hd, PL),
            jnp.transpose(cache_v, (0, 1, 3, 4, 2)).reshape(depth, nb, H * hd, PL),
            jnp.transpose(cache_logf, (0, 1, 3, 2)))
    for l in range(depth):
        p = {k: v[l] for k, v in stacked.items()}
        xp, st, kv_p = _mixer_and_moe(xp, p, None, None, None, H, l, depth, kv_p)
        st_p.append(st)
        xs, st, kv_s = _mixer_and_moe(xs, p, cache_conv[l], state_ssm[l], past, H, l, depth, kv_s)
        st_s.append(st)
    stack = lambda sts, i: jnp.stack([s[i] for s in sts])

    def kv_out(a):
        return a if a.ndim == 5 else jnp.transpose(a.reshape(a.shape[0], a.shape[1], H, hd, a.shape[3]), (0, 1, 4, 2, 3))

    return ((xp, xs) + (stack(st_p, 0), stack(st_p, 1), kv_out(kv_p[0]), kv_out(kv_p[1]), stack(st_p, 2))
            + (stack(st_s, 0), stack(st_s, 1), kv_out(kv_s[0]), kv_out(kv_s[1]), stack(st_s, 2)))
```

```python
import functools
import math

import numpy as np
import jax
import jax.numpy as jnp
from jax import lax
from jax.experimental import pallas as pl
from jax.experimental.pallas import tpu as pltpu

F32 = jnp.float32
BF16 = jnp.bfloat16
EPS = 1e-6
NEG = -1e30
LANES = 128
CONV_HALO = 32
SSM_CHUNK = 16
MXU_DIM = 256
BIAS_SPLIT = 3
VMEM_LIMIT_BYTES = 56 * 1024 * 1024


def _cparams(*sem):
    return pltpu.CompilerParams(dimension_semantics=sem, vmem_limit_bytes=VMEM_LIMIT_BYTES)


def _tile(n, pref, mult=8):
    if n <= pref:
        return n
    t = (pref // mult) * mult
    while t >= mult:
        if n % t == 0:
            return t
        t -= mult
    return n


def _const_spec(shape):
    nd = len(shape)
    return pl.BlockSpec(shape, lambda *_: (0,) * nd)


def _in_proj_kernel(x_ref, g_ref, w_ref, b_ref, wf_ref, bf_ref, z_ref, lf_ref, xn_ref):
    @pl.when(pl.program_id(1) == 0)
    def _():
        x = x_ref[...]
        r = lax.rsqrt(jnp.mean(x * x, axis=-1, keepdims=True) + EPS)
        xn = (x * r * g_ref[...]).astype(BF16)
        xn_ref[...] = xn
        zf = jnp.dot(xn, wf_ref[...], preferred_element_type=F32) + bf_ref[...]
        lf_ref[...] = jnp.minimum(zf, 0.0) - jnp.log(1.0 + jnp.exp(-jnp.abs(zf)))

    z = jnp.dot(xn_ref[...], w_ref[...], preferred_element_type=F32) + b_ref[...]
    z_ref[...] = z.astype(z_ref.dtype)


def _in_proj(x, g, w, b, wf, bf):
    T, D = x.shape
    N = w.shape[1]
    tm = _tile(T, 1024)
    tn = _tile(N, 1536, LANES)
    return pl.pallas_call(
        _in_proj_kernel,
        grid=(T // tm, N // tn),
        in_specs=[
            pl.BlockSpec((tm, D), lambda i, j: (i, 0)),
            pl.BlockSpec((1, D), lambda i, j: (0, 0)),
            pl.BlockSpec((D, tn), lambda i, j: (0, j)),
            pl.BlockSpec((1, tn), lambda i, j: (0, j)),
            pl.BlockSpec((D, LANES), lambda i, j: (0, 0)),
            pl.BlockSpec((1, LANES), lambda i, j: (0, 0)),
        ],
        out_specs=[
            pl.BlockSpec((tm, tn), lambda i, j: (i, j)),
            pl.BlockSpec((tm, LANES), lambda i, j: (i, 0)),
        ],
        out_shape=[jax.ShapeDtypeStruct((T, N), BF16), jax.ShapeDtypeStruct((T, LANES), F32)],
        scratch_shapes=[pltpu.VMEM((tm, D), BF16)],
        compiler_params=_cparams("parallel", "arbitrary"),
        name="in_proj",
    )(x, g, w, b, wf, bf)


def _conv_kernel(a_ref, b_ref, buf0_ref, w_ref, bdw_ref, lng_ref, lnb_ref, h_ref, nc_ref, u_ref, acc_ref,
                 *, tt, C, K):
    ti = pl.program_id(1)

    @pl.when(ti == 0)
    def _():
        u_ref[0:CONV_HALO, :] = buf0_ref[0]

    @pl.when(ti > 0)
    def _():
        u_ref[0:CONV_HALO, :] = u_ref[tt:tt + CONV_HALO, :]

    u_ref[CONV_HALO:CONV_HALO + tt, :] = a_ref[...].astype(F32) * jax.nn.sigmoid(b_ref[...].astype(F32))
    base = CONV_HALO - (K - 1)
    rc = min(tt, 128)
    for r0 in range(0, tt, rc):
        for c0 in range(0, C, LANES):
            acc = jnp.zeros((rc, LANES), F32)
            for j in range(K):
                acc = acc + w_ref[j:j + 1, c0:c0 + LANES] * u_ref[r0 + base + j:r0 + base + j + rc, c0:c0 + LANES]
            acc_ref[r0:r0 + rc, c0:c0 + LANES] = acc
    h = acc_ref[...] + bdw_ref[...]
    mu = jnp.mean(h, axis=-1, keepdims=True)
    d = h - mu
    var = jnp.mean(d * d, axis=-1, keepdims=True)
    y = d * lax.rsqrt(var + EPS) * lng_ref[...] + lnb_ref[...]
    h_ref[...] = (y * jax.nn.sigmoid(y)).astype(h_ref.dtype)

    @pl.when(ti == pl.num_programs(1) - 1)
    def _():
        nc_ref[0] = u_ref[tt:tt + CONV_HALO, :]


def _conv_branch(z, col0, B, L, buf0, w_dw, b_dw, ln_g, ln_b):
    K, C = w_dw.shape
    assert K - 1 <= CONV_HALO and L >= CONV_HALO and C % LANES == 0 and col0 % C == 0
    tt = _tile(L, 256)
    nt = L // tt
    cb = col0 // C
    wpad = jnp.zeros((CONV_HALO, C), F32).at[:K].set(w_dw)
    kern = functools.partial(_conv_kernel, tt=tt, C=C, K=K)
    return pl.pallas_call(
        kern,
        grid=(B, nt),
        in_specs=[
            pl.BlockSpec((tt, C), lambda b, t: (b * nt + t, cb)),
            pl.BlockSpec((tt, C), lambda b, t: (b * nt + t, cb + 1)),
            pl.BlockSpec((1, CONV_HALO, C), lambda b, t: (b, 0, 0)),
            _const_spec((CONV_HALO, C)),
            _const_spec((1, C)), _const_spec((1, C)), _const_spec((1, C)),
        ],
        out_specs=[
            pl.BlockSpec((tt, C), lambda b, t: (b * nt + t, 0)),
            pl.BlockSpec((1, CONV_HALO, C), lambda b, t: (b, 0, 0)),
        ],
        out_shape=[jax.ShapeDtypeStruct((B * L, C), BF16), jax.ShapeDtypeStruct((B, CONV_HALO, C), F32)],
        scratch_shapes=[pltpu.VMEM((CONV_HALO + tt, C), F32), pltpu.VMEM((tt, C), F32)],
        compiler_params=_cparams("parallel", "arbitrary"),
        name="conv_branch",
    )(z, z, buf0, wpad, b_dw[None], ln_g[None], ln_b[None])


def _ssm_matrices(a_re, a_im, log_dt, b_re, b_im, c_re, c_im):
    hp = lax.Precision.HIGHEST
    G, P, gw = b_re.shape
    S = SSM_CHUNK
    lam_re = jnp.minimum(a_re, -1e-4)
    lam_im = a_im
    dt = jnp.exp(log_dt)[:, None]
    d = jnp.arange(S + 1, dtype=F32)[None, :, None]
    mag = jnp.exp(lam_re[:, None, :] * dt[:, None, :] * d)
    ang = lam_im[:, None, :] * dt[:, None, :] * d
    pw_re, pw_im = mag * jnp.cos(ang), mag * jnp.sin(ang)
    den = lam_re * lam_re + lam_im * lam_im
    xr, xi = pw_re[:, 1] - 1.0, pw_im[:, 1]
    cf_re = (xr * lam_re + xi * lam_im) / den
    cf_im = (xi * lam_re - xr * lam_im) / den
    bb_re = cf_re[..., None] * b_re - cf_im[..., None] * b_im
    bb_im = cf_re[..., None] * b_im + cf_im[..., None] * b_re
    cp_re = c_re[:, None] * pw_re[:, :, None, :] - c_im[:, None] * pw_im[:, :, None, :]
    cp_im = c_re[:, None] * pw_im[:, :, None, :] + c_im[:, None] * pw_re[:, :, None, :]
    kk = (jnp.einsum('gdcp,gpe->gdce', cp_re[:, :S], bb_re, precision=hp)
          - jnp.einsum('gdcp,gpe->gdce', cp_im[:, :S], bb_im, precision=hp))
    s_idx = jnp.arange(S)[:, None]
    t_idx = jnp.arange(S)[None, :]
    lag = jnp.clip(t_idx - s_idx, 0, S - 1)
    m = kk[:, lag] * (t_idx >= s_idx)[None, :, :, None, None]
    m = m.transpose(0, 1, 4, 2, 3).reshape(G, S * gw, S * gw)
    rev_re = pw_re[:, S - 1 - jnp.arange(S)]
    rev_im = pw_im[:, S - 1 - jnp.arange(S)]
    p_re = rev_re[:, :, None, :] * bb_re.transpose(0, 2, 1)[:, None] - rev_im[:, :, None, :] * bb_im.transpose(0, 2, 1)[:, None]
    p_im = rev_re[:, :, None, :] * bb_im.transpose(0, 2, 1)[:, None] + rev_im[:, :, None, :] * bb_re.transpose(0, 2, 1)[:, None]
    p_re = p_re.reshape(G, S * gw, P)
    p_im = p_im.reshape(G, S * gw, P)
    w1 = jnp.concatenate([m, p_re, p_im, p_im, p_re], axis=-1)
    q_re = cp_re[:, 1:].transpose(0, 3, 1, 2).reshape(G, P, S * gw)
    q_im = -cp_im[:, 1:].transpose(0, 3, 1, 2).reshape(G, P, S * gw)
    q = jnp.concatenate([q_re, q_im], axis=1)
    ar, ai = pw_re[:, S], pw_im[:, S]
    a = jnp.zeros((G, 8, 2 * P), F32)
    a = a.at[:, 0].set(jnp.concatenate([ar, ar], -1))
    a = a.at[:, 1].set(jnp.concatenate([-ai, ai], -1))
    a = a.at[:, 2].set(jnp.concatenate([ai, -ai], -1))
    return w1.astype(BF16), q.astype(BF16), a


def _ssm_kernel(u_ref, w1_ref, q_ref, a_ref, h0_ref, h0s_ref, y_ref, hf_ref, tmp_ref, hh_ref, *, nk, B, CW, P2):
    tmp_ref[...] = jnp.dot(u_ref[0], w1_ref[0], preferred_element_type=F32)
    a1 = a_ref[0, 0:1, :]
    a2 = a_ref[0, 1:2, :]
    a2s = a_ref[0, 2:3, :]

    def body(k, carry):
        h, hs = carry
        r = pl.multiple_of(k * B, B)
        hh_ref[pl.ds(r, B), :] = h
        inc = tmp_ref[pl.ds(r, B), CW:CW + P2]
        incs = tmp_ref[pl.ds(r, B), CW + P2:CW + 2 * P2]
        return a1 * h + a2 * hs + inc, a1 * hs + a2s * h + incs

    h, _ = lax.fori_loop(0, nk, body, (h0_ref[0], h0s_ref[0]))
    hf_ref[0] = h
    y = tmp_ref[:, 0:CW] + jnp.dot(hh_ref[...].astype(BF16), q_ref[0], preferred_element_type=F32)
    y_ref[0] = y.astype(y_ref.dtype)


def _ssm_branch(u, B, L, mats, h0):
    w1, q, a = mats
    G, CW, _ = w1.shape
    P2 = q.shape[1]
    gw = CW // SSM_CHUNK
    assert CW == MXU_DIM and L % SSM_CHUNK == 0
    nk = L // SSM_CHUNK
    R = nk * B
    ug = u.reshape(B, nk, SSM_CHUNK, G, gw).transpose(3, 1, 0, 2, 4).reshape(G, R, CW)
    h0r = jnp.concatenate([h0[..., 0], h0[..., 1]], -1).transpose(1, 0, 2)
    h0s = jnp.concatenate([h0[..., 1], h0[..., 0]], -1).transpose(1, 0, 2)
    kern = functools.partial(_ssm_kernel, nk=nk, B=B, CW=CW, P2=P2)
    y, hf = pl.pallas_call(
        kern,
        grid=(G,),
        in_specs=[
            pl.BlockSpec((1, R, CW), lambda g: (g, 0, 0)),
            pl.BlockSpec((1, CW, CW + 2 * P2), lambda g: (g, 0, 0)),
            pl.BlockSpec((1, P2, CW), lambda g: (g, 0, 0)),
            pl.BlockSpec((1, 8, P2), lambda g: (g, 0, 0)),
            pl.BlockSpec((1, B, P2), lambda g: (g, 0, 0)),
            pl.BlockSpec((1, B, P2), lambda g: (g, 0, 0)),
        ],
        out_specs=[
            pl.BlockSpec((1, R, CW), lambda g: (g, 0, 0)),
            pl.BlockSpec((1, B, P2), lambda g: (g, 0, 0)),
        ],
        out_shape=[jax.ShapeDtypeStruct((G, R, CW), BF16), jax.ShapeDtypeStruct((G, B, P2), F32)],
        scratch_shapes=[pltpu.VMEM((R, CW + 2 * P2), F32), pltpu.VMEM((R, P2), F32)],
        compiler_params=_cparams("parallel"),
        name="ssm_branch",
    )(ug, w1, q, a, h0r, h0s)
    y = y.reshape(G, nk, B, SSM_CHUNK, gw).transpose(2, 1, 3, 0, 4).reshape(B * L, G * gw)
    P = P2 // 2
    new_state = jnp.stack([hf[..., :P], hf[..., P:]], axis=-1).transpose(1, 0, 2, 3)
    return y, new_state


def _ssm_seq_kernel(u_ref, w1_ref, q_ref, a_ref, h0_ref, h0s_ref, y_ref, hf_ref, uf_ref, tmp_ref, hh_ref, yg_ref,
                    *, nk, G, gw, CW, P2):
    S = SSM_CHUNK
    per = LANES // gw
    lane = lax.broadcasted_iota(jnp.int32, (1, LANES), 1)
    slot = [(lane >= i * gw) & (lane < (i + 1) * gw) for i in range(per)]
    for j in range(G // per):
        uf_ref[j] = u_ref[:, j * LANES:(j + 1) * LANES].astype(F32)

    for j in range(G // per):
        xs = [uf_ref[j, pl.ds(s, nk, stride=S), :] for s in range(S)]
        for g8 in range(per):
            g = j * per + g8
            halves = []
            for d in range(S // per):
                acc = jnp.zeros((nk, LANES), F32)
                for s8 in range(per):
                    x = xs[d * per + s8]
                    sh = ((s8 - g8) * gw) % LANES
                    acc = jnp.where(slot[s8], pltpu.roll(x, sh, axis=1) if sh else x, acc)
                halves.append(acc)
            ug = jnp.concatenate(halves, axis=1).astype(BF16)
            tmp_ref[:, g, :] = jnp.dot(ug, w1_ref[g], preferred_element_type=F32)

    a1, a2, a2s = a_ref[0], a_ref[1], a_ref[2]

    def body(k, carry):
        h, hs = carry
        hh_ref[k] = h
        inc = tmp_ref[k, :, CW:CW + P2]
        incs = tmp_ref[k, :, CW + P2:CW + 2 * P2]
        return a1 * h + a2 * hs + inc, a1 * hs + a2s * h + incs

    h, _ = lax.fori_loop(0, nk, body, (h0_ref[0], h0s_ref[0]))
    hf_ref[0] = h

    for g in range(G):
        yg_ref[g] = tmp_ref[:, g, 0:CW] + jnp.dot(hh_ref[:, g, :].astype(BF16), q_ref[g], preferred_element_type=F32)

    for j in range(G // per):
        for t in range(S):
            d, t8 = divmod(t, per)
            acc = jnp.zeros((nk, LANES), F32)
            for g8 in range(per):
                x = yg_ref[j * per + g8, :, d * LANES:(d + 1) * LANES]
                sh = ((g8 - t8) * gw) % LANES
                acc = jnp.where(slot[g8], pltpu.roll(x, sh, axis=1) if sh else x, acc)
            uf_ref[j, pl.ds(t, nk, stride=S), :] = acc
    for j in range(G // per):
        y_ref[:, j * LANES:(j + 1) * LANES] = uf_ref[j].astype(y_ref.dtype)


def _ssm_branch_natural(z, ucol, B, L, mats, h0):
    w1, q, a = mats
    G, CW, _ = w1.shape
    P2 = q.shape[1]
    gw = CW // SSM_CHUNK
    SW = G * gw
    nk = L // SSM_CHUNK
    assert CW == MXU_DIM and L % SSM_CHUNK == 0 and nk % 8 == 0 and LANES % gw == 0 and SW % LANES == 0
    assert SSM_CHUNK % (LANES // gw) == 0 and ucol % SW == 0
    h0r = jnp.concatenate([h0[..., 0], h0[..., 1]], -1)
    h0s = jnp.concatenate([h0[..., 1], h0[..., 0]], -1)
    a3 = a[:, :3].transpose(1, 0, 2)
    kern = functools.partial(_ssm_seq_kernel, nk=nk, G=G, gw=gw, CW=CW, P2=P2)
    single = pl.Buffered(1)
    y, hf = pl.pallas_call(
        kern,
        grid=(B,),
        in_specs=[
            pl.BlockSpec((L, SW), lambda b: (b, ucol // SW)),
            pl.BlockSpec(w1.shape, lambda b: (0, 0, 0), pipeline_mode=single),
            pl.BlockSpec(q.shape, lambda b: (0, 0, 0), pipeline_mode=single),
            pl.BlockSpec(a3.shape, lambda b: (0, 0, 0)),
            pl.BlockSpec((1, G, P2), lambda b: (b, 0, 0)),
            pl.BlockSpec((1, G, P2), lambda b: (b, 0, 0)),
        ],
        out_specs=[
            pl.BlockSpec((L, SW), lambda b: (b, 0)),
            pl.BlockSpec((1, G, P2), lambda b: (b, 0, 0)),
        ],
        out_shape=[jax.ShapeDtypeStruct((B * L, SW), BF16), jax.ShapeDtypeStruct((B, G, P2), F32)],
        scratch_shapes=[pltpu.VMEM((SW // LANES, L, LANES), F32), pltpu.VMEM((nk, G, CW + 2 * P2), F32),
                        pltpu.VMEM((nk, G, P2), F32), pltpu.VMEM((G, nk, CW), F32)],
        compiler_params=_cparams("parallel"),
        name="ssm_branch_seq",
    )(z, w1, q, a3, h0r, h0s)
    P = P2 // 2
    return y, jnp.stack([hf[..., :P], hf[..., P:]], axis=-1)


def _bias_table(H):
    assert H % 2 == 0 and BIAS_SPLIT * H <= LANES
    pk = np.zeros((LANES, (H // 2) * LANES), np.float32)
    for h in range(H):
        p, e = divmod(h, 2)
        for i in range(BIAS_SPLIT):
            pk[i * H + h, p * LANES + e * BIAS_SPLIT + i] = -1.0
    return jnp.asarray(pk, BF16)


def _stack_pair_queries(q, e, hd):
    lane = lax.broadcasted_iota(jnp.int32, (1, LANES), 1)
    qe = jnp.where((lane >= e * hd) & (lane < (e + 1) * hd), q, jnp.zeros_like(q))
    one = jnp.where((lane >= e * BIAS_SPLIT) & (lane < (e + 1) * BIAS_SPLIT), 1.0, 0.0).astype(q.dtype)
    return jnp.concatenate([qe, jnp.broadcast_to(one, q.shape)], axis=1)


def _split3(x):
    hi = x.astype(BF16)
    r1 = x - hi.astype(F32)
    mid = r1.astype(BF16)
    lo = (r1 - mid.astype(F32)).astype(BF16)
    return hi, mid, lo


def _forget_cumsum(lf, carry, H):
    tl = lf.shape[0]
    lane = lax.broadcasted_iota(jnp.int32, (1, LANES), 1)
    lf = jnp.where(lane < H, lf, 0.0)
    tri = (lax.broadcasted_iota(jnp.int32, (tl, tl), 0) >= lax.broadcasted_iota(jnp.int32, (tl, tl), 1)).astype(BF16)
    hi, mid, lo = _split3(lf)
    c = (jnp.dot(tri, hi, preferred_element_type=F32) + jnp.dot(tri, mid, preferred_element_type=F32)
         + jnp.dot(tri, lo, preferred_element_type=F32))
    return c + carry


def _pack_split(c, H):
    hi, mid, lo = _split3(c)
    packed = hi.astype(F32) + pltpu.roll(mid.astype(F32), H, axis=1) + pltpu.roll(lo.astype(F32), 2 * H, axis=1)
    return packed.astype(BF16)


def _pair_rmsnorm(x, gain, lo_half, hd):
    sq = x * x
    s_lo = jnp.sum(jnp.where(lo_half, sq, 0.0), axis=-1, keepdims=True)
    s_hi = jnp.sum(jnp.where(lo_half, 0.0, sq), axis=-1, keepdims=True)
    ms = jnp.where(lo_half, s_lo, s_hi) * (1.0 / hd)
    return x * lax.rsqrt(ms + EPS) * gain


def _qk_kernel(*refs, tl, H, hd, scale, aliased, time_minor, own):
    zq_ref, zk_ref, zv_ref, lf_ref, qg_ref, kg_ref, pk_ref = refs[:7]
    qa_ref, ka_ref, k5_ref, v5_ref, c_ref = refs[9:] if aliased else refs[7:]

    @pl.when(pl.program_id(1) == 0)
    def _():
        c_ref[...] = jnp.zeros_like(c_ref)

    for other in range(k5_ref.shape[0]):
        if other != own:
            k5_ref[other] = jnp.zeros(k5_ref.shape[1:], F32)
            v5_ref[other] = jnp.zeros(v5_ref.shape[1:], F32)

    c = _forget_cumsum(lf_ref[...], c_ref[0:1, :], H)
    c_ref[...] = jnp.broadcast_to(c[tl - 1:tl, :], c_ref.shape)
    bk = jnp.dot(_pack_split(c, H), pk_ref[...], preferred_element_type=F32)
    lo_half = lax.broadcasted_iota(jnp.int32, (1, LANES), 1) < hd
    for p in range(H // 2):
        sl = slice(p * LANES, (p + 1) * LANES)
        qn = _pair_rmsnorm(zq_ref[:, sl].astype(F32), qg_ref[:, sl], lo_half, hd) * scale
        kn = _pair_rmsnorm(zk_ref[:, sl].astype(F32), kg_ref[:, sl], lo_half, hd)
        vv = zv_ref[:, sl].astype(F32)
        qa_ref[:, sl] = qn.astype(BF16)
        ka_ref[:, 2 * p * LANES:(2 * p + 1) * LANES] = kn.astype(BF16)
        ka_ref[:, (2 * p + 1) * LANES:(2 * p + 2) * LANES] = bk[:, sl].astype(BF16)
        if time_minor:
            k5_ref[own, 0, sl, :] = kn.T
            v5_ref[own, 0, sl, :] = vv.T
        else:
            for e in range(2):
                k5_ref[own, 0, :, 2 * p + e, :] = kn[:, e * hd:(e + 1) * hd]
                v5_ref[own, 0, :, 2 * p + e, :] = vv[:, e * hd:(e + 1) * hd]


def _qk_prep(z, qcol, B, L, logf, q_gain, k_gain, H, layer, depth, kv_prev):
    hd = q_gain.shape[0]
    AW = H * hd
    assert 2 * hd == LANES and qcol % AW == 0
    tl = _tile(L, 256)
    nt = L // tl
    qb = qcol // AW
    pk = _bias_table(H)
    NA = H * LANES
    aliased = kv_prev is not None
    time_minor = tl % LANES == 0
    nl, l0 = (1, layer) if aliased else (depth, 0)
    kern = functools.partial(_qk_kernel, tl=tl, H=H, hd=hd, scale=hd ** -0.5, aliased=aliased,
                             time_minor=time_minor, own=layer - l0)
    row = lambda b, t: b * nt + t
    in_specs = [
        pl.BlockSpec((tl, AW), lambda b, t: (row(b, t), qb)),
        pl.BlockSpec((tl, AW), lambda b, t: (row(b, t), qb + 1)),
        pl.BlockSpec((tl, AW), lambda b, t: (row(b, t), qb + 2)),
        pl.BlockSpec((tl, LANES), lambda b, t: (row(b, t), 0)),
        _const_spec((1, AW)), _const_spec((1, AW)), _const_spec(pk.shape),
    ]
    args = [z, z, z, logf, jnp.tile(q_gain, H)[None], jnp.tile(k_gain, H)[None], pk]
    if aliased:
        in_specs += [pl.BlockSpec(memory_space=pl.ANY), pl.BlockSpec(memory_space=pl.ANY)]
        args += list(kv_prev)
    if time_minor:
        kv_spec = pl.BlockSpec((nl, 1, AW, tl), lambda b, t: (l0, b, 0, t))
        kv_shape = jax.ShapeDtypeStruct((depth, B, AW, L), F32)
    else:
        kv_spec = pl.BlockSpec((nl, 1, tl, H, hd), lambda b, t: (l0, b, t, 0, 0))
        kv_shape = jax.ShapeDtypeStruct((depth, B, L, H, hd), F32)
    return pl.pallas_call(
        kern,
        grid=(B, nt),
        in_specs=in_specs,
        out_specs=[
            pl.BlockSpec((tl, AW), lambda b, t: (row(b, t), 0)),
            pl.BlockSpec((tl, NA), lambda b, t: (row(b, t), 0)),
            kv_spec, kv_spec,
        ],
        out_shape=[jax.ShapeDtypeStruct((B * L, AW), BF16), jax.ShapeDtypeStruct((B * L, NA), BF16),
                   kv_shape, kv_shape],
        scratch_shapes=[pltpu.VMEM((8, LANES), F32)],
        input_output_aliases={7: 2, 8: 3} if aliased else {},
        compiler_params=_cparams("parallel", "arbitrary"),
        name="qk_prep",
    )(*args)


def _attn_kernel(q_ref, k_ref, v_ref, o_ref, q2_ref, acc_ref, m_ref, *, tq, tk, hd):
    qi = pl.program_id(2)
    q = q_ref[...]
    for e in range(2):
        q2_ref[e * tq:(e + 1) * tq, :] = _stack_pair_queries(q, e, hd)
    q_lo = qi * tq
    n_full = (q_lo + 1) // tk
    n_tot = (q_lo + tq + tk - 1) // tk
    m_ref[...] = jnp.full(m_ref.shape, NEG, F32)
    acc_ref[...] = jnp.zeros(acc_ref.shape, F32)
    ones = jnp.ones((tk, LANES), BF16)

    def step(j, masked):
        k0 = pl.multiple_of(j * tk, tk)
        mask = None
        if masked:
            row = lax.broadcasted_iota(jnp.int32, (2 * tq, 1), 0)
            qpos = q_lo + jnp.where(row >= tq, row - tq, row)
            mask = k0 + lax.broadcasted_iota(jnp.int32, (1, tk), 1) <= qpos
        vs = jnp.concatenate([v_ref[pl.ds(k0, tk), :], ones], axis=1)
        _softmax_step(q2_ref[...], k_ref[pl.ds(k0, tk), :], vs, mask, m_ref, acc_ref)

    def full_body(j, c):
        step(j, False)
        return c

    def edge_body(j, c):
        step(j, True)
        return c

    lax.fori_loop(0, n_full, full_body, 0)
    lax.fori_loop(n_full, n_tot, edge_body, 0)
    o_ref[...] = _pair_output(acc_ref[...], tq, hd).astype(o_ref.dtype)


def _softmax_step(q2, k_aug, v_aug, mask, m_ref, acc_ref):
    tk = k_aug.shape[0]
    s = lax.dot_general(q2, k_aug, (((1,), (1,)), ((), ())), preferred_element_type=F32)
    if mask is not None:
        s = jnp.where(mask, s, NEG)
    m_prev = m_ref[...]
    m_next = jnp.maximum(m_prev, jnp.max(s, axis=-1, keepdims=True))
    alpha = jnp.exp(m_prev - m_next)
    p = jnp.exp((s - jnp.concatenate([m_next] * (tk // LANES), axis=1)).astype(BF16))
    pv = jnp.dot(p, v_aug, preferred_element_type=F32)
    acc_ref[...] = jnp.concatenate([alpha, alpha], axis=1) * acc_ref[...] + pv
    m_ref[...] = m_next


def _pair_output(acc, tq, hd):
    o = acc[:, 0:LANES] / acc[:, LANES:2 * LANES]
    lane = lax.broadcasted_iota(jnp.int32, (1, LANES), 1)
    return jnp.where(lane < hd, o[0:tq], o[tq:2 * tq])


def _attention(qa, ka, v, vcol, B, L, H, hd, tq, tk):
    npair = H // 2
    nq = L // tq
    assert L % tq == 0 and L % tk == 0 and tk % LANES == 0 and vcol % LANES == 0
    vb = vcol // LANES
    kern = functools.partial(_attn_kernel, tq=tq, tk=tk, hd=hd)
    return pl.pallas_call(
        kern,
        grid=(B, npair, nq),
        in_specs=[
            pl.BlockSpec((tq, LANES), lambda b, p, i: (b * nq + i, p)),
            pl.BlockSpec((L, 2 * LANES), lambda b, p, i: (b, p)),
            pl.BlockSpec((L, LANES), lambda b, p, i: (b, vb + p)),
        ],
        out_specs=pl.BlockSpec((tq, LANES), lambda b, p, i: (b * nq + i, p)),
        out_shape=jax.ShapeDtypeStruct((B * L, H * hd), BF16),
        scratch_shapes=[pltpu.VMEM((2 * tq, 2 * LANES), BF16), pltpu.VMEM((2 * tq, 2 * LANES), F32),
                        pltpu.VMEM((2 * tq, LANES), F32)],
        compiler_params=_cparams("parallel", "parallel", "arbitrary"),
        name="fox_attention",
    )(qa, ka, v)


def _cached_attn_kernel(q_ref, kc_ref, vc_ref, lf_ref, kn_ref, vn_ref, o_ref,
                        q2_ref, acc_ref, m_ref, l_ref, c_ref, *, Lq, tl, H, hd):
    t = pl.program_id(1)
    npair = H // 2
    row = lax.broadcasted_iota(jnp.int32, (2 * Lq, 1), 0)

    @pl.when(t == 0)
    def _():
        c_ref[...] = jnp.zeros_like(c_ref)
        m_ref[...] = jnp.full(m_ref.shape, NEG, F32)
        l_ref[...] = jnp.zeros(l_ref.shape, F32)
        acc_ref[...] = jnp.zeros(acc_ref.shape, F32)
        for p in range(npair):
            q = q_ref[:, p * LANES:(p + 1) * LANES]
            for e in range(2):
                q2_ref[p, e * Lq:(e + 1) * Lq, :] = _stack_pair_queries(q, e, hd)

    def update(p, s, weigh):
        m_prev = m_ref[p]
        m_next = jnp.maximum(m_prev, jnp.max(s, axis=-1, keepdims=True))
        alpha = jnp.exp(m_prev - m_next)
        pr = jnp.exp(s - jnp.concatenate([m_next] * (s.shape[1] // LANES), axis=1))
        l_ref[p] = alpha * l_ref[p] + jnp.sum(pr, axis=-1, keepdims=True)
        acc_ref[p] = alpha * acc_ref[p] + weigh(pr.astype(BF16))
        m_ref[p] = m_next

    lf = lf_ref[0, 0]
    after = (lax.broadcasted_iota(jnp.int32, (tl, tl), 0) > lax.broadcasted_iota(jnp.int32, (tl, tl), 1)).astype(BF16)
    later = sum(jnp.dot(piece, after, preferred_element_type=F32) for piece in _split3(lf))
    later = later + jnp.concatenate([c_ref[...]] * (tl // LANES), axis=1)
    c_ref[...] = c_ref[...] + jnp.sum(lf, axis=-1, keepdims=True)
    nt_dims = (((1,), (1,)), ((), ()))
    for p in range(npair):
        sl = slice(p * LANES, (p + 1) * LANES)
        kp = kc_ref[0, 0, sl, :].astype(BF16)
        vp = vc_ref[0, 0, sl, :].astype(BF16)
        s = jnp.dot(q2_ref[p, :, 0:LANES], kp, preferred_element_type=F32)
        s = s + jnp.where(row < Lq, later[2 * p:2 * p + 1, :], later[2 * p + 1:2 * p + 2, :])
        update(p, s, lambda pr: lax.dot_general(pr, vp, nt_dims, preferred_element_type=F32))

    @pl.when(t == pl.num_programs(1) - 1)
    def _():
        nk = kn_ref.shape[0]
        mask = lax.broadcasted_iota(jnp.int32, (1, nk), 1) <= jnp.where(row >= Lq, row - Lq, row)
        lane = lax.broadcasted_iota(jnp.int32, (1, LANES), 1)
        for p in range(npair):
            s = lax.dot_general(q2_ref[p], kn_ref[:, 2 * p * LANES:(2 * p + 2) * LANES], nt_dims,
                                preferred_element_type=F32)
            vn = vn_ref[:, p * LANES:(p + 1) * LANES]
            update(p, jnp.where(mask, s, NEG), lambda pr: jnp.dot(pr, vn, preferred_element_type=F32))
            o = acc_ref[p] / l_ref[p]
            o_ref[:, p * LANES:(p + 1) * LANES] = jnp.where(lane < hd, o[0:Lq], o[Lq:2 * Lq]).astype(o_ref.dtype)


def _cached_attention(qa, ka_new, v_new, cache_kt, cache_vt, cache_lft, layer, B, Lq, H, hd):
    PL = cache_kt.shape[-1]
    AW = H * hd
    NA = H * LANES
    nk = ka_new.shape[0] // B
    tl = _tile(PL, 1024, LANES)
    nt = PL // tl
    npair = H // 2
    kern = functools.partial(_cached_attn_kernel, Lq=Lq, tl=tl, H=H, hd=hd)
    kv_spec = pl.BlockSpec((1, 1, AW, tl), lambda b, t: (layer, b, 0, nt - 1 - t))
    return pl.pallas_call(
        kern,
        grid=(B, nt),
        in_specs=[
            pl.BlockSpec((Lq, AW), lambda b, t: (b, 0)),
            kv_spec, kv_spec,
            pl.BlockSpec((1, 1, H, tl), lambda b, t: (layer, b, 0, nt - 1 - t)),
            pl.BlockSpec((nk, NA), lambda b, t: (b, 0)),
            pl.BlockSpec((nk, AW), lambda b, t: (b, 0)),
        ],
        out_specs=pl.BlockSpec((Lq, AW), lambda b, t: (b, 0)),
        out_shape=jax.ShapeDtypeStruct((B * Lq, AW), BF16),
        scratch_shapes=[pltpu.VMEM((npair, 2 * Lq, 2 * LANES), BF16), pltpu.VMEM((npair, 2 * Lq, LANES), F32),
                        pltpu.VMEM((npair, 2 * Lq, LANES), F32), pltpu.VMEM((npair, 2 * Lq, LANES), F32),
                        pltpu.VMEM((H, LANES), F32)],
        compiler_params=_cparams("parallel", "arbitrary"),
        name="fox_cached_attention",
    )(qa, cache_kt, cache_vt, cache_lft, ka_new, v_new)


def _merge_kernel(x_ref, ha_ref, ys_ref, u_ref, o_ref, ga_ref, gb_ref, gc_ref, d_ref, wglu_ref, bglu_ref,
                  wa_ref, wb_ref, wc_ref, wo_ref, out_ref):
    ys = ys_ref[...].astype(F32) + d_ref[...] * u_ref[...].astype(F32)
    hs = 0.5 * ys * (1.0 + jnp.tanh(math.sqrt(2.0 / math.pi) * (ys + 0.044715 * (ys * ys * ys))))
    gl = jnp.dot(hs.astype(BF16), wglu_ref[...], preferred_element_type=F32) + bglu_ref[...]
    hs = hs * jax.nn.sigmoid(gl)
    oa = jnp.dot(ha_ref[...], wa_ref[...], preferred_element_type=F32)
    ob = jnp.dot(hs.astype(BF16), wb_ref[...], preferred_element_type=F32)
    oc = jnp.dot(o_ref[...], wc_ref[...], preferred_element_type=F32)
    m = (jax.nn.sigmoid(ga_ref[...].astype(F32)) * oa + jax.nn.sigmoid(gb_ref[...].astype(F32)) * ob
         + jax.nn.sigmoid(gc_ref[...].astype(F32)) * oc)
    out_ref[...] = x_ref[...] + jnp.dot(m.astype(BF16), wo_ref[...], preferred_element_type=F32)


def _merge(x, ha, ys, z, ucol, gcol, o, d, wglu, bglu, wa, wb, wc, wo):
    T, D = x.shape
    C, S, AW = ha.shape[1], ys.shape[1], o.shape[1]
    assert ucol % S == 0 and gcol % D == 0
    tm = _tile(T, 256)
    ub, gb = ucol // S, gcol // D
    single = pl.Buffered(1)
    wspec = lambda a: pl.BlockSpec(a.shape, lambda i: (0, 0), pipeline_mode=single)
    return pl.pallas_call(
        _merge_kernel,
        grid=(T // tm,),
        in_specs=[
            pl.BlockSpec((tm, D), lambda i: (i, 0)),
            pl.BlockSpec((tm, C), lambda i: (i, 0)),
            pl.BlockSpec((tm, S), lambda i: (i, 0)),
            pl.BlockSpec((tm, S), lambda i: (i, ub)),
            pl.BlockSpec((tm, AW), lambda i: (i, 0)),
            pl.BlockSpec((tm, D), lambda i: (i, gb)),
            pl.BlockSpec((tm, D), lambda i: (i, gb + 1)),
            pl.BlockSpec((tm, D), lambda i: (i, gb + 2)),
            _const_spec((1, S)), wspec(wglu), _const_spec((1, S)),
            wspec(wa), wspec(wb), wspec(wc), wspec(wo),
        ],
        out_specs=pl.BlockSpec((tm, D), lambda i: (i, 0)),
        out_shape=jax.ShapeDtypeStruct((T, D), F32),
        compiler_params=_cparams("parallel"),
        name="merge",
    )(x, ha, ys, z, o, z, z, z, d[None], wglu, bglu[None], wa, wb, wc, wo)


def _route_kernel(x_ref, g_ref, wh_ref, wl_ref, b_ref, r_ref, *, ng, epg):
    x = x_ref[...]
    xn = x * lax.rsqrt(jnp.mean(x * x, axis=-1, keepdims=True) + EPS) * g_ref[...]
    xh = xn.astype(BF16)
    xl = (xn - xh.astype(F32)).astype(BF16)
    wh = wh_ref[...]
    lg = (jnp.dot(xh, wh, preferred_element_type=F32) + jnp.dot(xl, wh, preferred_element_type=F32)
          + jnp.dot(xh, wl_ref[...], preferred_element_type=F32) + b_ref[...])
    lane = lax.broadcasted_iota(jnp.int32, (1, LANES), 1).astype(F32)
    big = float(LANES)
    is_g = lane < ng
    gmax = jnp.max(jnp.where(is_g, lg, NEG), axis=-1, keepdims=True)
    den = jnp.sum(jnp.where(is_g, jnp.exp(jnp.where(is_g, lg, NEG) - gmax), 0.0), axis=-1, keepdims=True)
    p_top = 1.0 / den
    g_top = jnp.min(jnp.where(is_g & (lg == gmax), lane, big), axis=-1, keepdims=True)
    sel = (lane >= ng + g_top * epg) & (lane < ng + (g_top + 1.0) * epg)
    le = jnp.where(sel, lg, NEG)
    v1 = jnp.max(le, axis=-1, keepdims=True)
    i1 = jnp.min(jnp.where(sel & (le == v1), lane, big), axis=-1, keepdims=True)
    sel2 = sel & (lane != i1)
    le2 = jnp.where(sel2, lg, NEG)
    v2 = jnp.max(le2, axis=-1, keepdims=True)
    i2 = jnp.min(jnp.where(sel2 & (le2 == v2), lane, big), axis=-1, keepdims=True)
    t = jnp.exp(v2 - v1)
    w1 = p_top / (1.0 + t)
    w2 = p_top * t / (1.0 + t)
    r_ref[...] = jnp.where(lane == 0, i1 - ng, jnp.where(lane == 1, i2 - ng, jnp.where(lane == 2, w1, jnp.where(lane == 3, w2, 0.0))))


def _route(x, g, w_route_g, b_route_g, w_route_e, b_route_e):
    T, D = x.shape
    ng, ne = w_route_g.shape[1], w_route_e.shape[1]
    assert ng + ne <= LANES
    w = jnp.zeros((D, LANES), F32).at[:, :ng].set(w_route_g).at[:, ng:ng + ne].set(w_route_e)
    b = jnp.zeros((1, LANES), F32).at[0, :ng].set(b_route_g).at[0, ng:ng + ne].set(b_route_e)
    wh = w.astype(BF16)
    wl = (w - wh.astype(F32)).astype(BF16)
    tm = _tile(T, 512)
    return pl.pallas_call(
        functools.partial(_route_kernel, ng=ng, epg=ne // ng),
        grid=(T // tm,),
        in_specs=[pl.BlockSpec((tm, D), lambda i: (i, 0)), _const_spec((1, D)),
                  _const_spec((D, LANES)), _const_spec((D, LANES)), _const_spec((1, LANES))],
        out_specs=pl.BlockSpec((tm, LANES), lambda i: (i, 0)),
        out_shape=jax.ShapeDtypeStruct((T, LANES), F32),
        compiler_params=_cparams("parallel"),
        name="route",
    )(x, g[None], wh, wl, b)


def _dispatch_plan(route, n_exp, tmo):
    T = route.shape[0]
    eid = route[:, 0:2].astype(jnp.int32).reshape(-1)
    onehot = (eid[:, None] == jnp.arange(n_exp)[None, :]).astype(jnp.int32)
    csum = jnp.cumsum(onehot, axis=0)
    rank = jnp.sum((csum - onehot) * onehot, axis=1)
    counts = csum[-1]
    padded = ((counts + tmo - 1) // tmo) * tmo
    ends = jnp.cumsum(padded)
    starts = ends - padded
    pos = starts[eid] + rank
    NP = 2 * T + n_exp * tmo
    tok = jnp.zeros((NP,), jnp.int32).at[pos].set(jnp.arange(2 * T, dtype=jnp.int32) // 2)
    tile_start = jnp.arange(NP // tmo, dtype=jnp.int32) * tmo
    tile_exp = jnp.minimum(jnp.sum((tile_start[:, None] >= ends[None, :]).astype(jnp.int32), axis=1), n_exp - 1)
    n_used = (ends[-1] // tmo).astype(jnp.int32).reshape(1)
    return tok, tile_exp, n_used, pos.astype(jnp.int32)


def _pack_bf16_halves(y):
    half = y.shape[1] // 2
    bits = lambda a: lax.bitcast_convert_type(a.astype(BF16).astype(F32), jnp.uint32)
    return (bits(y[:, :half]) >> 16) | (bits(y[:, half:]) & jnp.uint32(0xFFFF0000))


def _unpack_bf16_halves(w):
    lo = lax.bitcast_convert_type(w << 16, F32)
    hi = lax.bitcast_convert_type(w & jnp.uint32(0xFFFF0000), F32)
    return lo, hi


def _moe_kernel(te_ref, tok_ref, nu_ref, x_hbm, g_ref, wg_ref, wu_ref, wd_ref, y_ref, xbuf, sem, *, tmo):
    i = pl.program_id(0)
    n_used = nu_ref[0]
    slot = i % 2

    def row_copy(tile, dst_slot, r):
        t = tok_ref[tile * tmo + r]
        return pltpu.make_async_copy(x_hbm.at[pl.ds(t, 1), :], xbuf.at[dst_slot, pl.ds(r, 1), :], sem.at[dst_slot])

    def experts():
        pltpu.make_async_copy(x_hbm.at[pl.ds(0, tmo), :], xbuf.at[slot], sem.at[slot]).wait()
        x = xbuf[slot]
        xn = (x * lax.rsqrt(jnp.mean(x * x, axis=-1, keepdims=True) + EPS) * g_ref[...]).astype(BF16)
        hg = jnp.dot(xn, wg_ref[0], preferred_element_type=F32)
        hu = jnp.dot(xn, wu_ref[0], preferred_element_type=F32)
        h = (hg * jax.nn.sigmoid(hg) * hu).astype(BF16)
        y_ref[...] = _pack_bf16_halves(jnp.dot(h, wd_ref[0], preferred_element_type=F32))

    @pl.when(i == 0)
    def _():
        def issue(r, c):
            row_copy(0, 0, r).start()
            return c
        lax.fori_loop(0, tmo, issue, 0)

    @pl.when(i + 1 < n_used)
    def _():
        for r in range(tmo):
            row_copy(i + 1, 1 - slot, r).start()
        experts()

    @pl.when(i + 1 == n_used)
    def _():
        experts()

    @pl.when(i >= n_used)
    def _():
        y_ref[...] = jnp.zeros_like(y_ref)


def _moe(x, g, plan, wg, wu, wd, tmo):
    T, D = x.shape
    tok, tile_exp, n_used, _ = plan
    NP = tok.shape[0]
    E, _, DE = wg.shape
    grid_spec = pltpu.PrefetchScalarGridSpec(
        num_scalar_prefetch=3,
        grid=(NP // tmo,),
        in_specs=[
            pl.BlockSpec(memory_space=pl.ANY),
            pl.BlockSpec((1, D), lambda i, te, tk, nu: (0, 0)),
            pl.BlockSpec((1, D, DE), lambda i, te, tk, nu: (te[i], 0, 0)),
            pl.BlockSpec((1, D, DE), lambda i, te, tk, nu: (te[i], 0, 0)),
            pl.BlockSpec((1, DE, D), lambda i, te, tk, nu: (te[i], 0, 0)),
        ],
        out_specs=pl.BlockSpec((tmo, D // 2), lambda i, te, tk, nu: (i, 0)),
        scratch_shapes=[pltpu.VMEM((2, tmo, D), F32), pltpu.SemaphoreType.DMA((2,))],
    )
    return pl.pallas_call(
        functools.partial(_moe_kernel, tmo=tmo),
        grid_spec=grid_spec,
        out_shape=jax.ShapeDtypeStruct((NP, D // 2), jnp.uint32),
        compiler_params=_cparams("arbitrary"),
        name="moe_experts",
    )(tile_exp, tok, n_used, x, g[None], wg, wu, wd)


def _combine_kernel(pos_ref, x_ref, r_ref, y_hbm, out_ref, ybuf, sem, *, tc):
    i = pl.program_id(0)
    n = pl.num_programs(0)
    slot = i % 2

    def row_copy(tile, dst_slot, r, s):
        p = pos_ref[2 * (tile * tc + r) + s]
        return pltpu.make_async_copy(y_hbm.at[pl.ds(p, 1), :], ybuf.at[dst_slot, s, pl.ds(r, 1), :], sem.at[dst_slot])

    @pl.when(i == 0)
    def _():
        def issue(r, c):
            for s in range(2):
                row_copy(0, 0, r, s).start()
            return c
        lax.fori_loop(0, tc, issue, 0)

    @pl.when(i + 1 < n)
    def _():
        for r in range(tc):
            for s in range(2):
                row_copy(i + 1, 1 - slot, r, s).start()

    for s in range(2):
        pltpu.make_async_copy(y_hbm.at[pl.ds(0, tc), :], ybuf.at[slot, s], sem.at[slot]).wait()
    half = out_ref.shape[1] // 2
    lo0, hi0 = _unpack_bf16_halves(ybuf[slot, 0])
    lo1, hi1 = _unpack_bf16_halves(ybuf[slot, 1])
    w0 = r_ref[:, 2:3]
    w1 = r_ref[:, 3:4]
    out_ref[:, :half] = x_ref[:, :half] + (w0 * lo0 + w1 * lo1)
    out_ref[:, half:] = x_ref[:, half:] + (w0 * hi0 + w1 * hi1)


def _combine(x, route, y_sorted, pos):
    T, D = x.shape
    tc = _tile(T, 256)
    grid_spec = pltpu.PrefetchScalarGridSpec(
        num_scalar_prefetch=1,
        grid=(T // tc,),
        in_specs=[pl.BlockSpec((tc, D), lambda i, p: (i, 0)), pl.BlockSpec((tc, LANES), lambda i, p: (i, 0)),
                  pl.BlockSpec(memory_space=pl.ANY)],
        out_specs=pl.BlockSpec((tc, D), lambda i, p: (i, 0)),
        scratch_shapes=[pltpu.VMEM((2, 2, tc, D // 2), jnp.uint32), pltpu.SemaphoreType.DMA((2,))],
    )
    return pl.pallas_call(
        functools.partial(_combine_kernel, tc=tc),
        grid_spec=grid_spec,
        out_shape=jax.ShapeDtypeStruct((T, D), F32),
        compiler_params=_cparams("arbitrary"),
        name="moe_combine",
    )(pos, x, route, y_sorted)


def _layer_weights(p, H):
    D = p['w_in'].shape[0]
    C = p['w_dw'].shape[1]
    S = p['ssm_d'].shape[0]
    AW = H * p['q_gain'].shape[0]
    o_u = 2 * C
    o_q = o_u + S
    o_f = o_q + 3 * AW
    o_g = o_f + H
    perm = lambda a: jnp.concatenate([a[..., o_g:], a[..., o_q:o_f], a[..., :o_u], a[..., o_u:o_q]], axis=-1)
    cols = dict(g=0, q=3 * D, conv=3 * D + 3 * AW, u=3 * D + 3 * AW + 2 * C)
    wf = jnp.zeros((D, LANES), F32).at[:, :H].set(p['w_in'][:, o_f:o_g]).astype(BF16)
    bf = jnp.zeros((1, LANES), F32).at[0, :H].set(p['b_in'][o_f:o_g])
    return perm(p['w_in']).astype(BF16), perm(p['b_in'])[None], wf, bf, cols


def _mixer_and_moe(x3, p, conv_buf, ssm_h0, past, H, layer, depth, kv_prev):
    B, L, D = x3.shape
    T = B * L
    x = x3.reshape(T, D)
    hd = p['q_gain'].shape[0]
    AW = H * hd
    K, C = p['w_dw'].shape
    G, P, gw = p['ssm_b_re'].shape
    S = G * gw
    w_main, b_main, wf, bf, cols = _layer_weights(p, H)
    z, logf = _in_proj(x, p['g_mix'][None], w_main, b_main, wf, bf)

    buf0 = jnp.zeros((B, CONV_HALO, C), F32)
    if conv_buf is not None:
        buf0 = buf0.at[:, CONV_HALO - (K - 1):].set(conv_buf)
    ha, nc = _conv_branch(z, cols['conv'], B, L, buf0, p['w_dw'], p['b_dw'], p['ln_g'], p['ln_b'])
    new_conv = nc[:, CONV_HALO - (K - 1):]

    mats = _ssm_matrices(p['ssm_a_re'], p['ssm_a_im'], p['ssm_log_dt'], p['ssm_b_re'], p['ssm_b_im'],
                         p['ssm_c_re'], p['ssm_c_im'])
    h0 = jnp.zeros((B, G, P, 2), F32) if ssm_h0 is None else ssm_h0
    if (L // SSM_CHUNK) % 8 == 0:
        ys, new_ssm = _ssm_branch_natural(z, cols['u'], B, L, mats, h0)
    else:
        ys, new_ssm = _ssm_branch(z[:, cols['u']:cols['u'] + S], B, L, mats, h0)

    vcol = cols['q'] + 2 * AW
    qa, ka, k5, v5 = _qk_prep(z, cols['q'], B, L, logf, p['q_gain'], p['k_gain'], H, layer, depth, kv_prev)
    if past is None:
        o = _attention(qa, ka, z, vcol, B, L, H, hd, _tile(L, 256), _tile(L, 512, LANES))
    else:
        nk = ((L + LANES - 1) // LANES) * LANES
        pad_rows = lambda a: jnp.pad(a.reshape(B, L, -1), ((0, 0), (0, nk - L), (0, 0))).reshape(B * nk, -1)
        o = _cached_attention(qa, pad_rows(ka), pad_rows(z[:, vcol:vcol + AW]), *past, layer, B, L, H, hd)

    bfw = lambda a: a.astype(BF16)
    x = _merge(x, ha, ys, z, cols['u'], cols['g'], o, p['ssm_d'], bfw(p['w_glu']), p['b_glu'],
               bfw(p['w_conv_out']), bfw(p['w_ssm_out']), bfw(p['w_attn_out']), bfw(p['w_out']))

    route = _route(x, p['g_ffn'], p['w_route_g'], p['b_route_g'], p['w_route_e'], p['b_route_e'])
    E = p['w_e_gate'].shape[0]
    tmo = _tile(T, 256 if 2 * T >= 256 * E else 64)
    plan = _dispatch_plan(route, E, tmo)
    y_sorted = _moe(x, p['g_ffn'], plan, bfw(p['w_e_gate']), bfw(p['w_e_up']), bfw(p['w_e_down']), tmo)
    x = _combine(x, route, y_sorted, plan[3])

    state = (new_conv, new_ssm, logf[:, :H].reshape(B, L, H))
    return x.reshape(B, L, D), state, (k5, v5)


def kernel(x_prompt, x_sample, cache_conv, state_ssm, cache_k, cache_v, cache_logf, g_mix, w_in, b_in, w_dw, b_dw, ln_g, ln_b, w_conv_out, ssm_a_re, ssm_a_im, ssm_log_dt, ssm_b_re, ssm_b_im, ssm_c_re, ssm_c_im, ssm_d, w_glu, b_glu, w_ssm_out, q_gain, k_gain, w_attn_out, w_out, g_ffn, w_route_g, b_route_g, w_route_e, b_route_e, w_e_gate, w_e_up, w_e_down):
    depth = w_in.shape[0]
    H = cache_logf.shape[-1]
    stacked = dict(g_mix=g_mix, w_in=w_in, b_in=b_in, w_dw=w_dw, b_dw=b_dw, ln_g=ln_g, ln_b=ln_b,
                   w_conv_out=w_conv_out, ssm_a_re=ssm_a_re, ssm_a_im=ssm_a_im, ssm_log_dt=ssm_log_dt,
                   ssm_b_re=ssm_b_re, ssm_b_im=ssm_b_im, ssm_c_re=ssm_c_re, ssm_c_im=ssm_c_im, ssm_d=ssm_d,
                   w_glu=w_glu, b_glu=b_glu, w_ssm_out=w_ssm_out, q_gain=q_gain, k_gain=k_gain,
                   w_attn_out=w_attn_out, w_out=w_out, g_ffn=g_ffn, w_route_g=w_route_g, b_route_g=b_route_g,
                   w_route_e=w_route_e, b_route_e=b_route_e, w_e_gate=w_e_gate, w_e_up=w_e_up, w_e_down=w_e_down)
    xp, xs = x_prompt, x_sample
    st_p, st_s = [], []
    kv_p = kv_s = None
    nb, PL, hd = cache_k.shape[1], cache_k.shape[2], cache_k.shape[4]
    past = (jnp.transpose(cache_k, (0, 1, 3, 4, 2)).reshape(depth, nb, H * hd, PL),
            jnp.transpose(cache_v, (0, 1, 3, 4, 2)).reshape(depth, nb, H * hd, PL),
            jnp.transpose(cache_logf, (0, 1, 3, 2)))
    for l in range(depth):
        p = {k: v[l] for k, v in stacked.items()}
        xp, st, kv_p = _mixer_and_moe(xp, p, None, None, None, H, l, depth, kv_p)
        st_p.append(st)
        xs, st, kv_s = _mixer_and_moe(xs, p, cache_conv[l], state_ssm[l], past, H, l, depth, kv_s)
        st_s.append(st)
    stack = lambda sts, i: jnp.stack([s[i] for s in sts])

    def kv_out(a):
        return a if a.ndim == 5 else jnp.transpose(a.reshape(a.shape[0], a.shape[1], H, hd, a.shape[3]), (0, 1, 4, 2, 3))

    return ((xp, xs) + (stack(st_p, 0), stack(st_p, 1), kv_out(kv_p[0]), kv_out(kv_p[1]), stack(st_p, 2))
            + (stack(st_s, 0), stack(st_s, 1), kv_out(kv_s[0]), kv_out(kv_s[1]), stack(st_s, 2)))
```

```python
import functools
import math

import numpy as np
import jax
import jax.numpy as jnp
from jax import lax
from jax.experimental import pallas as pl
from jax.experimental.pallas import tpu as pltpu

F32 = jnp.float32
BF16 = jnp.bfloat16
EPS = 1e-6
NEG = -1e30
LANES = 128
CONV_HALO = 32
SSM_CHUNK = 16
MXU_DIM = 256
BIAS_SPLIT = 3
VMEM_LIMIT_BYTES = 56 * 1024 * 1024


def _cparams(*sem):
    return pltpu.CompilerParams(dimension_semantics=sem, vmem_limit_bytes=VMEM_LIMIT_BYTES)


def _tile(n, pref, mult=8):
    if n <= pref:
        return n
    t = (pref // mult) * mult
    while t >= mult:
        if n % t == 0:
            return t
        t -= mult
    return n


def _const_spec(shape):
    nd = len(shape)
    return pl.BlockSpec(shape, lambda *_: (0,) * nd)


def _in_proj_kernel(x_ref, g_ref, w_ref, b_ref, wf_ref, bf_ref, z_ref, lf_ref, xn_ref):
    @pl.when(pl.program_id(1) == 0)
    def _():
        x = x_ref[...]
        r = lax.rsqrt(jnp.mean(x * x, axis=-1, keepdims=True) + EPS)
        xn = (x * r * g_ref[...]).astype(BF16)
        xn_ref[...] = xn
        zf = jnp.dot(xn, wf_ref[...], preferred_element_type=F32) + bf_ref[...]
        lf_ref[...] = jnp.minimum(zf, 0.0) - jnp.log(1.0 + jnp.exp(-jnp.abs(zf)))

    z = jnp.dot(xn_ref[...], w_ref[...], preferred_element_type=F32) + b_ref[...]
    z_ref[...] = z.astype(z_ref.dtype)


def _in_proj(x, g, w, b, wf, bf):
    T, D = x.shape
    N = w.shape[1]
    tm = _tile(T, 1024)
    tn = _tile(N, 1536, LANES)
    return pl.pallas_call(
        _in_proj_kernel,
        grid=(T // tm, N // tn),
        in_specs=[
            pl.BlockSpec((tm, D), lambda i, j: (i, 0)),
            pl.BlockSpec((1, D), lambda i, j: (0, 0)),
            pl.BlockSpec((D, tn), lambda i, j: (0, j)),
            pl.BlockSpec((1, tn), lambda i, j: (0, j)),
            pl.BlockSpec((D, LANES), lambda i, j: (0, 0)),
            pl.BlockSpec((1, LANES), lambda i, j: (0, 0)),
        ],
        out_specs=[
            pl.BlockSpec((tm, tn), lambda i, j: (i, j)),
            pl.BlockSpec((tm, LANES), lambda i, j: (i, 0)),
        ],
        out_shape=[jax.ShapeDtypeStruct((T, N), BF16), jax.ShapeDtypeStruct((T, LANES), F32)],
        scratch_shapes=[pltpu.VMEM((tm, D), BF16)],
        compiler_params=_cparams("parallel", "arbitrary"),
        name="in_proj",
    )(x, g, w, b, wf, bf)


def _conv_kernel(a_ref, b_ref, buf0_ref, w_ref, bdw_ref, lng_ref, lnb_ref, h_ref, nc_ref, u_ref, acc_ref,
                 *, tt, C, K):
    ti = pl.program_id(1)

    @pl.when(ti == 0)
    def _():
        u_ref[0:CONV_HALO, :] = buf0_ref[0]

    @pl.when(ti > 0)
    def _():
        u_ref[0:CONV_HALO, :] = u_ref[tt:tt + CONV_HALO, :]

    u_ref[CONV_HALO:CONV_HALO + tt, :] = a_ref[...].astype(F32) * jax.nn.sigmoid(b_ref[...].astype(F32))
    base = CONV_HALO - (K - 1)
    rc = min(tt, 128)
    for r0 in range(0, tt, rc):
        for c0 in range(0, C, LANES):
            acc = jnp.zeros((rc, LANES), F32)
            for j in range(K):
                acc = acc + w_ref[j:j + 1, c0:c0 + LANES] * u_ref[r0 + base + j:r0 + base + j + rc, c0:c0 + LANES]
            acc_ref[r0:r0 + rc, c0:c0 + LANES] = acc
    h = acc_ref[...] + bdw_ref[...]
    mu = jnp.mean(h, axis=-1, keepdims=True)
    d = h - mu
    var = jnp.mean(d * d, axis=-1, keepdims=True)
    y = d * lax.rsqrt(var + EPS) * lng_ref[...] + lnb_ref[...]
    h_ref[...] = (y * jax.nn.sigmoid(y)).astype(h_ref.dtype)

    @pl.when(ti == pl.num_programs(1) - 1)
    def _():
        nc_ref[0] = u_ref[tt:tt + CONV_HALO, :]


def _conv_branch(z, col0, B, L, buf0, w_dw, b_dw, ln_g, ln_b):
    K, C = w_dw.shape
    assert K - 1 <= CONV_HALO and L >= CONV_HALO and C % LANES == 0 and col0 % C == 0
    tt = _tile(L, 256)
    nt = L // tt
    cb = col0 // C
    wpad = jnp.zeros((CONV_HALO, C), F32).at[:K].set(w_dw)
    kern = functools.partial(_conv_kernel, tt=tt, C=C, K=K)
    return pl.pallas_call(
        kern,
        grid=(B, nt),
        in_specs=[
            pl.BlockSpec((tt, C), lambda b, t: (b * nt + t, cb)),
            pl.BlockSpec((tt, C), lambda b, t: (b * nt + t, cb + 1)),
            pl.BlockSpec((1, CONV_HALO, C), lambda b, t: (b, 0, 0)),
            _const_spec((CONV_HALO, C)),
            _const_spec((1, C)), _const_spec((1, C)), _const_spec((1, C)),
        ],
        out_specs=[
            pl.BlockSpec((tt, C), lambda b, t: (b * nt + t, 0)),
            pl.BlockSpec((1, CONV_HALO, C), lambda b, t: (b, 0, 0)),
        ],
        out_shape=[jax.ShapeDtypeStruct((B * L, C), BF16), jax.ShapeDtypeStruct((B, CONV_HALO, C), F32)],
        scratch_shapes=[pltpu.VMEM((CONV_HALO + tt, C), F32), pltpu.VMEM((tt, C), F32)],
        compiler_params=_cparams("parallel", "arbitrary"),
        name="conv_branch",
    )(z, z, buf0, wpad, b_dw[None], ln_g[None], ln_b[None])


def _ssm_matrices(a_re, a_im, log_dt, b_re, b_im, c_re, c_im):
    hp = lax.Precision.HIGHEST
    G, P, gw = b_re.shape
    S = SSM_CHUNK
    lam_re = jnp.minimum(a_re, -1e-4)
    lam_im = a_im
    dt = jnp.exp(log_dt)[:, None]
    d = jnp.arange(S + 1, dtype=F32)[None, :, None]
    mag = jnp.exp(lam_re[:, None, :] * dt[:, None, :] * d)
    ang = lam_im[:, None, :] * dt[:, None, :] * d
    pw_re, pw_im = mag * jnp.cos(ang), mag * jnp.sin(ang)
    den = lam_re * lam_re + lam_im * lam_im
    xr, xi = pw_re[:, 1] - 1.0, pw_im[:, 1]
    cf_re = (xr * lam_re + xi * lam_im) / den
    cf_im = (xi * lam_re - xr * lam_im) / den
    bb_re = cf_re[..., None] * b_re - cf_im[..., None] * b_im
    bb_im = cf_re[..., None] * b_im + cf_im[..., None] * b_re
    cp_re = c_re[:, None] * pw_re[:, :, None, :] - c_im[:, None] * pw_im[:, :, None, :]
    cp_im = c_re[:, None] * pw_im[:, :, None, :] + c_im[:, None] * pw_re[:, :, None, :]
    kk = (jnp.einsum('gdcp,gpe->gdce', cp_re[:, :S], bb_re, precision=hp)
          - jnp.einsum('gdcp,gpe->gdce', cp_im[:, :S], bb_im, precision=hp))
    s_idx = jnp.arange(S)[:, None]
    t_idx = jnp.arange(S)[None, :]
    lag = jnp.clip(t_idx - s_idx, 0, S - 1)
    m = kk[:, lag] * (t_idx >= s_idx)[None, :, :, None, None]
    m = m.transpose(0, 1, 4, 2, 3).reshape(G, S * gw, S * gw)
    rev_re = pw_re[:, S - 1 - jnp.arange(S)]
    rev_im = pw_im[:, S - 1 - jnp.arange(S)]
    p_re = rev_re[:, :, None, :] * bb_re.transpose(0, 2, 1)[:, None] - rev_im[:, :, None, :] * bb_im.transpose(0, 2, 1)[:, None]
    p_im = rev_re[:, :, None, :] * bb_im.transpose(0, 2, 1)[:, None] + rev_im[:, :, None, :] * bb_re.transpose(0, 2, 1)[:, None]
    p_re = p_re.reshape(G, S * gw, P)
    p_im = p_im.reshape(G, S * gw, P)
    w1 = jnp.concatenate([m, p_re, p_im, p_im, p_re], axis=-1)
    q_re = cp_re[:, 1:].transpose(0, 3, 1, 2).reshape(G, P, S * gw)
    q_im = -cp_im[:, 1:].transpose(0, 3, 1, 2).reshape(G, P, S * gw)
    q = jnp.concatenate([q_re, q_im], axis=1)
    ar, ai = pw_re[:, S], pw_im[:, S]
    a = jnp.zeros((G, 8, 2 * P), F32)
    a = a.at[:, 0].set(jnp.concatenate([ar, ar], -1))
    a = a.at[:, 1].set(jnp.concatenate([-ai, ai], -1))
    a = a.at[:, 2].set(jnp.concatenate([ai, -ai], -1))
    return w1.astype(BF16), q.astype(BF16), a


def _ssm_kernel(u_ref, w1_ref, q_ref, a_ref, h0_ref, h0s_ref, y_ref, hf_ref, tmp_ref, hh_ref, *, nk, B, CW, P2):
    tmp_ref[...] = jnp.dot(u_ref[0], w1_ref[0], preferred_element_type=F32)
    a1 = a_ref[0, 0:1, :]
    a2 = a_ref[0, 1:2, :]
    a2s = a_ref[0, 2:3, :]

    def body(k, carry):
        h, hs = carry
        r = pl.multiple_of(k * B, B)
        hh_ref[pl.ds(r, B), :] = h
        inc = tmp_ref[pl.ds(r, B), CW:CW + P2]
        incs = tmp_ref[pl.ds(r, B), CW + P2:CW + 2 * P2]
        return a1 * h + a2 * hs + inc, a1 * hs + a2s * h + incs

    h, _ = lax.fori_loop(0, nk, body, (h0_ref[0], h0s_ref[0]))
    hf_ref[0] = h
    y = tmp_ref[:, 0:CW] + jnp.dot(hh_ref[...].astype(BF16), q_ref[0], preferred_element_type=F32)
    y_ref[0] = y.astype(y_ref.dtype)


def _ssm_branch(u, B, L, mats, h0):
    w1, q, a = mats
    G, CW, _ = w1.shape
    P2 = q.shape[1]
    gw = CW // SSM_CHUNK
    assert CW == MXU_DIM and L % SSM_CHUNK == 0
    nk = L // SSM_CHUNK
    R = nk * B
    ug = u.reshape(B, nk, SSM_CHUNK, G, gw).transpose(3, 1, 0, 2, 4).reshape(G, R, CW)
    h0r = jnp.concatenate([h0[..., 0], h0[..., 1]], -1).transpose(1, 0, 2)
    h0s = jnp.concatenate([h0[..., 1], h0[..., 0]], -1).transpose(1, 0, 2)
    kern = functools.partial(_ssm_kernel, nk=nk, B=B, CW=CW, P2=P2)
    y, hf = pl.pallas_call(
        kern,
        grid=(G,),
        in_specs=[
            pl.BlockSpec((1, R, CW), lambda g: (g, 0, 0)),
            pl.BlockSpec((1, CW, CW + 2 * P2), lambda g: (g, 0, 0)),
            pl.BlockSpec((1, P2, CW), lambda g: (g, 0, 0)),
            pl.BlockSpec((1, 8, P2), lambda g: (g, 0, 0)),
            pl.BlockSpec((1, B, P2), lambda g: (g, 0, 0)),
            pl.BlockSpec((1, B, P2), lambda g: (g, 0, 0)),
        ],
        out_specs=[
            pl.BlockSpec((1, R, CW), lambda g: (g, 0, 0)),
            pl.BlockSpec((1, B, P2), lambda g: (g, 0, 0)),
        ],
        out_shape=[jax.ShapeDtypeStruct((G, R, CW), BF16), jax.ShapeDtypeStruct((G, B, P2), F32)],
        scratch_shapes=[pltpu.VMEM((R, CW + 2 * P2), F32), pltpu.VMEM((R, P2), F32)],
        compiler_params=_cparams("parallel"),
        name="ssm_branch",
    )(ug, w1, q, a, h0r, h0s)
    y = y.reshape(G, nk, B, SSM_CHUNK, gw).transpose(2, 1, 3, 0, 4).reshape(B * L, G * gw)
    P = P2 // 2
    new_state = jnp.stack([hf[..., :P], hf[..., P:]], axis=-1).transpose(1, 0, 2, 3)
    return y, new_state


def _ssm_seq_kernel(u_ref, w1_ref, q_ref, a_ref, h0_ref, h0s_ref, y_ref, hf_ref, uf_ref, tmp_ref, hh_ref, yg_ref,
                    *, nk, G, gw, CW, P2):
    S = SSM_CHUNK
    per = LANES // gw
    lane = lax.broadcasted_iota(jnp.int32, (1, LANES), 1)
    slot = [(lane >= i * gw) & (lane < (i + 1) * gw) for i in range(per)]
    for j in range(G // per):
        uf_ref[j] = u_ref[:, j * LANES:(j + 1) * LANES].astype(F32)

    for j in range(G // per):
        xs = [uf_ref[j, pl.ds(s, nk, stride=S), :] for s in range(S)]
        for g8 in range(per):
            g = j * per + g8
            halves = []
            for d in range(S // per):
                acc = jnp.zeros((nk, LANES), F32)
                for s8 in range(per):
                    x = xs[d * per + s8]
                    sh = ((s8 - g8) * gw) % LANES
                    acc = jnp.where(slot[s8], pltpu.roll(x, sh, axis=1) if sh else x, acc)
                halves.append(acc)
            ug = jnp.concatenate(halves, axis=1).astype(BF16)
            tmp_ref[:, g, :] = jnp.dot(ug, w1_ref[g], preferred_element_type=F32)

    a1, a2, a2s = a_ref[0], a_ref[1], a_ref[2]

    def body(k, carry):
        h, hs = carry
        hh_ref[k] = h
        inc = tmp_ref[k, :, CW:CW + P2]
        incs = tmp_ref[k, :, CW + P2:CW + 2 * P2]
        return a1 * h + a2 * hs + inc, a1 * hs + a2s * h + incs

    h, _ = lax.fori_loop(0, nk, body, (h0_ref[0], h0s_ref[0]))
    hf_ref[0] = h

    for g in range(G):
        yg_ref[g] = tmp_ref[:, g, 0:CW] + jnp.dot(hh_ref[:, g, :].astype(BF16), q_ref[g], preferred_element_type=F32)

    for j in range(G // per):
        for t in range(S):
            d, t8 = divmod(t, per)
            acc = jnp.zeros((nk, LANES), F32)
            for g8 in range(per):
                x = yg_ref[j * per + g8, :, d * LANES:(d + 1) * LANES]
                sh = ((g8 - t8) * gw) % LANES
                acc = jnp.where(slot[g8], pltpu.roll(x, sh, axis=1) if sh else x, acc)
            uf_ref[j, pl.ds(t, nk, stride=S), :] = acc
    for j in range(G // per):
        y_ref[:, j * LANES:(j + 1) * LANES] = uf_ref[j].astype(y_ref.dtype)


def _ssm_branch_natural(z, ucol, B, L, mats, h0):
    w1, q, a = mats
    G, CW, _ = w1.shape
    P2 = q.shape[1]
    gw = CW // SSM_CHUNK
    SW = G * gw
    nk = L // SSM_CHUNK
    assert CW == MXU_DIM and L % SSM_CHUNK == 0 and nk % 8 == 0 and LANES % gw == 0 and SW % LANES == 0
    assert SSM_CHUNK % (LANES // gw) == 0 and ucol % SW == 0
    h0r = jnp.concatenate([h0[..., 0], h0[..., 1]], -1)
    h0s = jnp.concatenate([h0[..., 1], h0[..., 0]], -1)
    a3 = a[:, :3].transpose(1, 0, 2)
    kern = functools.partial(_ssm_seq_kernel, nk=nk, G=G, gw=gw, CW=CW, P2=P2)
    single = pl.Buffered(1)
    y, hf = pl.pallas_call(
        kern,
        grid=(B,),
        in_specs=[
            pl.BlockSpec((L, SW), lambda b: (b, ucol // SW)),
            pl.BlockSpec(w1.shape, lambda b: (0, 0, 0), pipeline_mode=single),
            pl.BlockSpec(q.shape, lambda b: (0, 0, 0), pipeline_mode=single),
            pl.BlockSpec(a3.shape, lambda b: (0, 0, 0)),
            pl.BlockSpec((1, G, P2), lambda b: (b, 0, 0)),
            pl.BlockSpec((1, G, P2), lambda b: (b, 0, 0)),
        ],
        out_specs=[
            pl.BlockSpec((L, SW), lambda b: (b, 0)),
            pl.BlockSpec((1, G, P2), lambda b: (b, 0, 0)),
        ],
        out_shape=[jax.ShapeDtypeStruct((B * L, SW), BF16), jax.ShapeDtypeStruct((B, G, P2), F32)],
        scratch_shapes=[pltpu.VMEM((SW // LANES, L, LANES), F32), pltpu.VMEM((nk, G, CW + 2 * P2), F32),
                        pltpu.VMEM((nk, G, P2), F32), pltpu.VMEM((G, nk, CW), F32)],
        compiler_params=_cparams("parallel"),
        name="ssm_branch_seq",
    )(z, w1, q, a3, h0r, h0s)
    P = P2 // 2
    return y, jnp.stack([hf[..., :P], hf[..., P:]], axis=-1)


def _bias_table(H):
    assert H % 2 == 0 and BIAS_SPLIT * H <= LANES
    pk = np.zeros((LANES, (H // 2) * LANES), np.float32)
    for h in range(H):
        p, e = divmod(h, 2)
        for i in range(BIAS_SPLIT):
            pk[i * H + h, p * LANES + e * BIAS_SPLIT + i] = -1.0
    return jnp.asarray(pk, BF16)


def _stack_pair_queries(q, e, hd):
    lane = lax.broadcasted_iota(jnp.int32, (1, LANES), 1)
    qe = jnp.where((lane >= e * hd) & (lane < (e + 1) * hd), q, jnp.zeros_like(q))
    one = jnp.where((lane >= e * BIAS_SPLIT) & (lane < (e + 1) * BIAS_SPLIT), 1.0, 0.0).astype(q.dtype)
    return jnp.concatenate([qe, jnp.broadcast_to(one, q.shape)], axis=1)


def _split3(x):
    hi = x.astype(BF16)
    r1 = x - hi.astype(F32)
    mid = r1.astype(BF16)
    lo = (r1 - mid.astype(F32)).astype(BF16)
    return hi, mid, lo


def _forget_cumsum(lf, carry, H):
    tl = lf.shape[0]
    lane = lax.broadcasted_iota(jnp.int32, (1, LANES), 1)
    lf = jnp.where(lane < H, lf, 0.0)
    tri = (lax.broadcasted_iota(jnp.int32, (tl, tl), 0) >= lax.broadcasted_iota(jnp.int32, (tl, tl), 1)).astype(BF16)
    hi, mid, lo = _split3(lf)
    c = (jnp.dot(tri, hi, preferred_element_type=F32) + jnp.dot(tri, mid, preferred_element_type=F32)
         + jnp.dot(tri, lo, preferred_element_type=F32))
    return c + carry


def _pack_split(c, H):
    hi, mid, lo = _split3(c)
    packed = hi.astype(F32) + pltpu.roll(mid.astype(F32), H, axis=1) + pltpu.roll(lo.astype(F32), 2 * H, axis=1)
    return packed.astype(BF16)


def _pair_rmsnorm(x, gain, lo_half, hd):
    sq = x * x
    s_lo = jnp.sum(jnp.where(lo_half, sq, 0.0), axis=-1, keepdims=True)
    s_hi = jnp.sum(jnp.where(lo_half, 0.0, sq), axis=-1, keepdims=True)
    ms = jnp.where(lo_half, s_lo, s_hi) * (1.0 / hd)
    return x * lax.rsqrt(ms + EPS) * gain


def _qk_kernel(*refs, tl, H, hd, scale, aliased, time_minor, own):
    zq_ref, zk_ref, zv_ref, lf_ref, qg_ref, kg_ref, pk_ref = refs[:7]
    qa_ref, ka_ref, k5_ref, v5_ref, c_ref = refs[9:] if aliased else refs[7:]

    @pl.when(pl.program_id(1) == 0)
    def _():
        c_ref[...] = jnp.zeros_like(c_ref)

    for other in range(k5_ref.shape[0]):
        if other != own:
            k5_ref[other] = jnp.zeros(k5_ref.shape[1:], F32)
            v5_ref[other] = jnp.zeros(v5_ref.shape[1:], F32)

    c = _forget_cumsum(lf_ref[...], c_ref[0:1, :], H)
    c_ref[...] = jnp.broadcast_to(c[tl - 1:tl, :], c_ref.shape)
    bk = jnp.dot(_pack_split(c, H), pk_ref[...], preferred_element_type=F32)
    lo_half = lax.broadcasted_iota(jnp.int32, (1, LANES), 1) < hd
    for p in range(H // 2):
        sl = slice(p * LANES, (p + 1) * LANES)
        qn = _pair_rmsnorm(zq_ref[:, sl].astype(F32), qg_ref[:, sl], lo_half, hd) * scale
        kn = _pair_rmsnorm(zk_ref[:, sl].astype(F32), kg_ref[:, sl], lo_half, hd)
        vv = zv_ref[:, sl].astype(F32)
        qa_ref[:, sl] = qn.astype(BF16)
        ka_ref[:, 2 * p * LANES:(2 * p + 1) * LANES] = kn.astype(BF16)
        ka_ref[:, (2 * p + 1) * LANES:(2 * p + 2) * LANES] = bk[:, sl].astype(BF16)
        if time_minor:
            k5_ref[own, 0, sl, :] = kn.T
            v5_ref[own, 0, sl, :] = vv.T
        else:
            for e in range(2):
                k5_ref[own, 0, :, 2 * p + e, :] = kn[:, e * hd:(e + 1) * hd]
                v5_ref[own, 0, :, 2 * p + e, :] = vv[:, e * hd:(e + 1) * hd]


def _qk_prep(z, qcol, B, L, logf, q_gain, k_gain, H, layer, depth, kv_prev):
    hd = q_gain.shape[0]
    AW = H * hd
    assert 2 * hd == LANES and qcol % AW == 0
    tl = _tile(L, 256)
    nt = L // tl
    qb = qcol // AW
    pk = _bias_table(H)
    NA = H * LANES
    aliased = kv_prev is not None
    time_minor = tl % LANES == 0
    nl, l0 = (1, layer) if aliased else (depth, 0)
    kern = functools.partial(_qk_kernel, tl=tl, H=H, hd=hd, scale=hd ** -0.5, aliased=aliased,
                             time_minor=time_minor, own=layer - l0)
    row = lambda b, t: b * nt + t
    in_specs = [
        pl.BlockSpec((tl, AW), lambda b, t: (row(b, t), qb)),
        pl.BlockSpec((tl, AW), lambda b, t: (row(b, t), qb + 1)),
        pl.BlockSpec((tl, AW), lambda b, t: (row(b, t), qb + 2)),
        pl.BlockSpec((tl, LANES), lambda b, t: (row(b, t), 0)),
        _const_spec((1, AW)), _const_spec((1, AW)), _const_spec(pk.shape),
    ]
    args = [z, z, z, logf, jnp.tile(q_gain, H)[None], jnp.tile(k_gain, H)[None], pk]
    if aliased:
        in_specs += [pl.BlockSpec(memory_space=pl.ANY), pl.BlockSpec(memory_space=pl.ANY)]
        args += list(kv_prev)
    if time_minor:
        kv_spec = pl.BlockSpec((nl, 1, AW, tl), lambda b, t: (l0, b, 0, t))
        kv_shape = jax.ShapeDtypeStruct((depth, B, AW, L), F32)
    else:
        kv_spec = pl.BlockSpec((nl, 1, tl, H, hd), lambda b, t: (l0, b, t, 0, 0))
        kv_shape = jax.ShapeDtypeStruct((depth, B, L, H, hd), F32)
    return pl.pallas_call(
        kern,
        grid=(B, nt),
        in_specs=in_specs,
        out_specs=[
            pl.BlockSpec((tl, AW), lambda b, t: (row(b, t), 0)),
            pl.BlockSpec((tl, NA), lambda b, t: (row(b, t), 0)),
            kv_spec, kv_spec,
        ],
        out_shape=[jax.ShapeDtypeStruct((B * L, AW), BF16), jax.ShapeDtypeStruct((B * L, NA), BF16),
                   kv_shape, kv_shape],
        scratch_shapes=[pltpu.VMEM((8, LANES), F32)],
        input_output_aliases={7: 2, 8: 3} if aliased else {},
        compiler_params=_cparams("parallel", "arbitrary"),
        name="qk_prep",
    )(*args)


def _attn_kernel(q_ref, k_ref, v_ref, o_ref, q2_ref, acc_ref, m_ref, *, tq, tk, hd):
    qi = pl.program_id(2)
    npp = q2_ref.shape[0]
    for pp in range(npp):
        q = q_ref[:, pp * LANES:(pp + 1) * LANES]
        for e in range(2):
            q2_ref[pp, e * tq:(e + 1) * tq, :] = _stack_pair_queries(q, e, hd)
    q_lo = pl.multiple_of(qi * tq, tq)
    m_ref[...] = jnp.full(m_ref.shape, NEG, F32)
    acc_ref[...] = jnp.zeros(acc_ref.shape, F32)

    def step(k0, width, masked):
        mask = None
        if masked:
            row = lax.broadcasted_iota(jnp.int32, (2 * tq, 1), 0)
            qpos = q_lo + jnp.where(row >= tq, row - tq, row)
            mask = k0 + lax.broadcasted_iota(jnp.int32, (1, width), 1) <= qpos
        ones = jnp.ones((width, LANES), BF16)
        chains = [(q2_ref[pp], k_ref[pl.ds(k0, width), 2 * pp * LANES:(2 * pp + 2) * LANES],
                   jnp.concatenate([v_ref[pl.ds(k0, width), pp * LANES:(pp + 1) * LANES], ones], axis=1),
                   m_ref.at[pp], acc_ref.at[pp]) for pp in range(npp)]
        _softmax_step(chains, mask)

    def full_body(j, c):
        step(pl.multiple_of(j * tk, tk), tk, False)
        return c

    n_wide = q_lo // tk
    lax.fori_loop(0, n_wide, full_body, 0)
    if tk != tq:
        @pl.when(q_lo - n_wide * tk >= tq)
        def _():
            step(pl.multiple_of(n_wide * tk, tq), tq, False)
    step(q_lo, tq, True)
    for pp in range(npp):
        o_ref[:, pp * LANES:(pp + 1) * LANES] = _pair_output(acc_ref[pp], tq, hd).astype(o_ref.dtype)


def _softmax_step(chains, mask):
    nt_dims = (((1,), (1,)), ((), ()))
    scores = [lax.dot_general(q2, k_aug, nt_dims, preferred_element_type=F32) for q2, k_aug, _, _, _ in chains]
    probs = []
    for s, (_, k_aug, _, m_ref, _) in zip(scores, chains):
        if mask is not None:
            s = jnp.where(mask, s, NEG)
        m_prev = m_ref[...]
        m_next = jnp.maximum(m_prev, jnp.max(s, axis=-1, keepdims=True))
        alpha = jnp.exp(m_prev - m_next)
        p = jnp.exp((s - jnp.concatenate([m_next] * (k_aug.shape[0] // LANES), axis=1)).astype(BF16))
        m_ref[...] = m_next
        probs.append((p, alpha))
    for (p, alpha), (_, _, v_aug, _, acc_ref) in zip(probs, chains):
        pv = jnp.dot(p, v_aug, preferred_element_type=F32)
        acc_ref[...] = jnp.concatenate([alpha, alpha], axis=1) * acc_ref[...] + pv


def _pair_output(acc, tq, hd):
    o = acc[:, 0:LANES] / acc[:, LANES:2 * LANES]
    lane = lax.broadcasted_iota(jnp.int32, (1, LANES), 1)
    return jnp.where(lane < hd, o[0:tq], o[tq:2 * tq])


def _attention(qa, ka, v, vcol, B, L, H, hd, tq, tk):
    npair = H // 2
    npp = 4 if npair % 4 == 0 else (2 if npair % 2 == 0 else 1)
    nq = L // tq
    assert L % tq == 0 and tk in (tq, 2 * tq) and tq % LANES == 0 and vcol % (npp * LANES) == 0
    vb = vcol // (npp * LANES)
    kern = functools.partial(_attn_kernel, tq=tq, tk=tk, hd=hd)
    return pl.pallas_call(
        kern,
        grid=(B, npair // npp, nq),
        in_specs=[
            pl.BlockSpec((tq, npp * LANES), lambda b, p, i: (b * nq + i, p)),
            pl.BlockSpec((L, npp * 2 * LANES), lambda b, p, i: (b, p)),
            pl.BlockSpec((L, npp * LANES), lambda b, p, i: (b, vb + p)),
        ],
        out_specs=pl.BlockSpec((tq, npp * LANES), lambda b, p, i: (b * nq + i, p)),
        out_shape=jax.ShapeDtypeStruct((B * L, H * hd), BF16),
        scratch_shapes=[pltpu.VMEM((npp, 2 * tq, 2 * LANES), BF16), pltpu.VMEM((npp, 2 * tq, 2 * LANES), F32),
                        pltpu.VMEM((npp, 2 * tq, LANES), F32)],
        compiler_params=_cparams("parallel", "parallel", "arbitrary"),
        name="fox_attention",
    )(qa, ka, v)


def _cached_attn_kernel(q_ref, kc_ref, vc_ref, lf_ref, kn_ref, vn_ref, o_ref,
                        q2_ref, acc_ref, m_ref, l_ref, c_ref, *, Lq, tl, H, hd):
    t = pl.program_id(1)
    npair = H // 2
    row = lax.broadcasted_iota(jnp.int32, (2 * Lq, 1), 0)

    @pl.when(t == 0)
    def _():
        c_ref[...] = jnp.zeros_like(c_ref)
        m_ref[...] = jnp.full(m_ref.shape, NEG, F32)
        l_ref[...] = jnp.zeros(l_ref.shape, F32)
        acc_ref[...] = jnp.zeros(acc_ref.shape, F32)
        for p in range(npair):
            q = q_ref[:, p * LANES:(p + 1) * LANES]
            for e in range(2):
                q2_ref[p, e * Lq:(e + 1) * Lq, :] = _stack_pair_queries(q, e, hd)

    def update(p, s, weigh):
        m_prev = m_ref[p]
        m_next = jnp.maximum(m_prev, jnp.max(s, axis=-1, keepdims=True))
        alpha = jnp.exp(m_prev - m_next)
        pr = jnp.exp(s - jnp.concatenate([m_next] * (s.shape[1] // LANES), axis=1))
        l_ref[p] = alpha * l_ref[p] + jnp.sum(pr, axis=-1, keepdims=True)
        acc_ref[p] = alpha * acc_ref[p] + weigh(pr.astype(BF16))
        m_ref[p] = m_next

    lf = lf_ref[0, 0]
    after = (lax.broadcasted_iota(jnp.int32, (tl, tl), 0) > lax.broadcasted_iota(jnp.int32, (tl, tl), 1)).astype(BF16)
    later = sum(jnp.dot(piece, after, preferred_element_type=F32) for piece in _split3(lf))
    later = later + jnp.concatenate([c_ref[...]] * (tl // LANES), axis=1)
    c_ref[...] = c_ref[...] + jnp.sum(lf, axis=-1, keepdims=True)
    nt_dims = (((1,), (1,)), ((), ()))
    for p in range(npair):
        sl = slice(p * LANES, (p + 1) * LANES)
        kp = kc_ref[0, 0, sl, :].astype(BF16)
        vp = vc_ref[0, 0, sl, :].astype(BF16)
        s = jnp.dot(q2_ref[p, :, 0:LANES], kp, preferred_element_type=F32)
        s = s + jnp.where(row < Lq, later[2 * p:2 * p + 1, :], later[2 * p + 1:2 * p + 2, :])
        update(p, s, lambda pr: lax.dot_general(pr, vp, nt_dims, preferred_element_type=F32))

    @pl.when(t == pl.num_programs(1) - 1)
    def _():
        nk = kn_ref.shape[0]
        mask = lax.broadcasted_iota(jnp.int32, (1, nk), 1) <= jnp.where(row >= Lq, row - Lq, row)
        lane = lax.broadcasted_iota(jnp.int32, (1, LANES), 1)
        for p in range(npair):
            s = lax.dot_general(q2_ref[p], kn_ref[:, 2 * p * LANES:(2 * p + 2) * LANES], nt_dims,
                                preferred_element_type=F32)
            vn = vn_ref[:, p * LANES:(p + 1) * LANES]
            update(p, jnp.where(mask, s, NEG), lambda pr: jnp.dot(pr, vn, preferred_element_type=F32))
            o = acc_ref[p] / l_ref[p]
            o_ref[:, p * LANES:(p + 1) * LANES] = jnp.where(lane < hd, o[0:Lq], o[Lq:2 * Lq]).astype(o_ref.dtype)


def _cached_attention(qa, ka_new, v_new, cache_kt, cache_vt, cache_lft, layer, B, Lq, H, hd):
    PL = cache_kt.shape[-1]
    AW = H * hd
    NA = H * LANES
    nk = ka_new.shape[0] // B
    tl = _tile(PL, 1024, LANES)
    nt = PL // tl
    npair = H // 2
    kern = functools.partial(_cached_attn_kernel, Lq=Lq, tl=tl, H=H, hd=hd)
    kv_spec = pl.BlockSpec((1, 1, AW, tl), lambda b, t: (layer, b, 0, nt - 1 - t))
    return pl.pallas_call(
        kern,
        grid=(B, nt),
        in_specs=[
            pl.BlockSpec((Lq, AW), lambda b, t: (b, 0)),
            kv_spec, kv_spec,
            pl.BlockSpec((1, 1, H, tl), lambda b, t: (layer, b, 0, nt - 1 - t)),
            pl.BlockSpec((nk, NA), lambda b, t: (b, 0)),
            pl.BlockSpec((nk, AW), lambda b, t: (b, 0)),
        ],
        out_specs=pl.BlockSpec((Lq, AW), lambda b, t: (b, 0)),
        out_shape=jax.ShapeDtypeStruct((B * Lq, AW), BF16),
        scratch_shapes=[pltpu.VMEM((npair, 2 * Lq, 2 * LANES), BF16), pltpu.VMEM((npair, 2 * Lq, LANES), F32),
                        pltpu.VMEM((npair, 2 * Lq, LANES), F32), pltpu.VMEM((npair, 2 * Lq, LANES), F32),
                        pltpu.VMEM((H, LANES), F32)],
        compiler_params=_cparams("parallel", "arbitrary"),
        name="fox_cached_attention",
    )(qa, cache_kt, cache_vt, cache_lft, ka_new, v_new)


def _merge_kernel(x_ref, ha_ref, ys_ref, u_ref, o_ref, ga_ref, gb_ref, gc_ref, d_ref, wglu_ref, bglu_ref,
                  wa_ref, wb_ref, wc_ref, wo_ref, out_ref):
    ys = ys_ref[...].astype(F32) + d_ref[...] * u_ref[...].astype(F32)
    hs = 0.5 * ys * (1.0 + jnp.tanh(math.sqrt(2.0 / math.pi) * (ys + 0.044715 * (ys * ys * ys))))
    gl = jnp.dot(hs.astype(BF16), wglu_ref[...], preferred_element_type=F32) + bglu_ref[...]
    hs = hs * jax.nn.sigmoid(gl)
    oa = jnp.dot(ha_ref[...], wa_ref[...], preferred_element_type=F32)
    ob = jnp.dot(hs.astype(BF16), wb_ref[...], preferred_element_type=F32)
    oc = jnp.dot(o_ref[...], wc_ref[...], preferred_element_type=F32)
    m = (jax.nn.sigmoid(ga_ref[...].astype(F32)) * oa + jax.nn.sigmoid(gb_ref[...].astype(F32)) * ob
         + jax.nn.sigmoid(gc_ref[...].astype(F32)) * oc)
    out_ref[...] = x_ref[...] + jnp.dot(m.astype(BF16), wo_ref[...], preferred_element_type=F32)


def _merge(x, ha, ys, z, ucol, gcol, o, d, wglu, bglu, wa, wb, wc, wo):
    T, D = x.shape
    C, S, AW = ha.shape[1], ys.shape[1], o.shape[1]
    assert ucol % S == 0 and gcol % D == 0
    tm = _tile(T, 256)
    ub, gb = ucol // S, gcol // D
    single = pl.Buffered(1)
    wspec = lambda a: pl.BlockSpec(a.shape, lambda i: (0, 0), pipeline_mode=single)
    return pl.pallas_call(
        _merge_kernel,
        grid=(T // tm,),
        in_specs=[
            pl.BlockSpec((tm, D), lambda i: (i, 0)),
            pl.BlockSpec((tm, C), lambda i: (i, 0)),
            pl.BlockSpec((tm, S), lambda i: (i, 0)),
            pl.BlockSpec((tm, S), lambda i: (i, ub)),
            pl.BlockSpec((tm, AW), lambda i: (i, 0)),
            pl.BlockSpec((tm, D), lambda i: (i, gb)),
            pl.BlockSpec((tm, D), lambda i: (i, gb + 1)),
            pl.BlockSpec((tm, D), lambda i: (i, gb + 2)),
            _const_spec((1, S)), wspec(wglu), _const_spec((1, S)),
            wspec(wa), wspec(wb), wspec(wc), wspec(wo),
        ],
        out_specs=pl.BlockSpec((tm, D), lambda i: (i, 0)),
        out_shape=jax.ShapeDtypeStruct((T, D), F32),
        compiler_params=_cparams("parallel"),
        name="merge",
    )(x, ha, ys, z, o, z, z, z, d[None], wglu, bglu[None], wa, wb, wc, wo)


def _route_kernel(x_ref, g_ref, wh_ref, wl_ref, b_ref, r_ref, *, ng, epg):
    x = x_ref[...]
    xn = x * lax.rsqrt(jnp.mean(x * x, axis=-1, keepdims=True) + EPS) * g_ref[...]
    xh = xn.astype(BF16)
    xl = (xn - xh.astype(F32)).astype(BF16)
    wh = wh_ref[...]
    lg = (jnp.dot(xh, wh, preferred_element_type=F32) + jnp.dot(xl, wh, preferred_element_type=F32)
          + jnp.dot(xh, wl_ref[...], preferred_element_type=F32) + b_ref[...])
    lane = lax.broadcasted_iota(jnp.int32, (1, LANES), 1).astype(F32)
    big = float(LANES)
    is_g = lane < ng
    gmax = jnp.max(jnp.where(is_g, lg, NEG), axis=-1, keepdims=True)
    den = jnp.sum(jnp.where(is_g, jnp.exp(jnp.where(is_g, lg, NEG) - gmax), 0.0), axis=-1, keepdims=True)
    p_top = 1.0 / den
    g_top = jnp.min(jnp.where(is_g & (lg == gmax), lane, big), axis=-1, keepdims=True)
    sel = (lane >= ng + g_top * epg) & (lane < ng + (g_top + 1.0) * epg)
    le = jnp.where(sel, lg, NEG)
    v1 = jnp.max(le, axis=-1, keepdims=True)
    i1 = jnp.min(jnp.where(sel & (le == v1), lane, big), axis=-1, keepdims=True)
    sel2 = sel & (lane != i1)
    le2 = jnp.where(sel2, lg, NEG)
    v2 = jnp.max(le2, axis=-1, keepdims=True)
    i2 = jnp.min(jnp.where(sel2 & (le2 == v2), lane, big), axis=-1, keepdims=True)
    t = jnp.exp(v2 - v1)
    w1 = p_top / (1.0 + t)
    w2 = p_top * t / (1.0 + t)
    r_ref[...] = jnp.where(lane == 0, i1 - ng, jnp.where(lane == 1, i2 - ng, jnp.where(lane == 2, w1, jnp.where(lane == 3, w2, 0.0))))


def _route(x, g, w_route_g, b_route_g, w_route_e, b_route_e):
    T, D = x.shape
    ng, ne = w_route_g.shape[1], w_route_e.shape[1]
    assert ng + ne <= LANES
    w = jnp.zeros((D, LANES), F32).at[:, :ng].set(w_route_g).at[:, ng:ng + ne].set(w_route_e)
    b = jnp.zeros((1, LANES), F32).at[0, :ng].set(b_route_g).at[0, ng:ng + ne].set(b_route_e)
    wh = w.astype(BF16)
    wl = (w - wh.astype(F32)).astype(BF16)
    tm = _tile(T, 512)
    return pl.pallas_call(
        functools.partial(_route_kernel, ng=ng, epg=ne // ng),
        grid=(T // tm,),
        in_specs=[pl.BlockSpec((tm, D), lambda i: (i, 0)), _const_spec((1, D)),
                  _const_spec((D, LANES)), _const_spec((D, LANES)), _const_spec((1, LANES))],
        out_specs=pl.BlockSpec((tm, LANES), lambda i: (i, 0)),
        out_shape=jax.ShapeDtypeStruct((T, LANES), F32),
        compiler_params=_cparams("parallel"),
        name="route",
    )(x, g[None], wh, wl, b)


def _dispatch_plan(route, n_exp, tmo):
    T = route.shape[0]
    eid = route[:, 0:2].astype(jnp.int32).reshape(-1)
    onehot = (eid[:, None] == jnp.arange(n_exp)[None, :]).astype(jnp.int32)
    csum = jnp.cumsum(onehot, axis=0)
    rank = jnp.sum((csum - onehot) * onehot, axis=1)
    counts = csum[-1]
    padded = ((counts + tmo - 1) // tmo) * tmo
    ends = jnp.cumsum(padded)
    starts = ends - padded
    pos = starts[eid] + rank
    NP = 2 * T + n_exp * tmo
    tok = jnp.zeros((NP,), jnp.int32).at[pos].set(jnp.arange(2 * T, dtype=jnp.int32) // 2)
    tile_start = jnp.arange(NP // tmo, dtype=jnp.int32) * tmo
    tile_exp = jnp.minimum(jnp.sum((tile_start[:, None] >= ends[None, :]).astype(jnp.int32), axis=1), n_exp - 1)
    n_used = (ends[-1] // tmo).astype(jnp.int32).reshape(1)
    return tok, tile_exp, n_used, pos.astype(jnp.int32)


def _pack_bf16_halves(y):
    half = y.shape[1] // 2
    bits = lambda a: lax.bitcast_convert_type(a.astype(BF16).astype(F32), jnp.uint32)
    return (bits(y[:, :half]) >> 16) | (bits(y[:, half:]) & jnp.uint32(0xFFFF0000))


def _unpack_bf16_halves(w):
    lo = lax.bitcast_convert_type(w << 16, F32)
    hi = lax.bitcast_convert_type(w & jnp.uint32(0xFFFF0000), F32)
    return lo, hi


def _moe_kernel(te_ref, tok_ref, nu_ref, x_hbm, g_ref, wg_ref, wu_ref, wd_ref, y_ref, xbuf, sem, *, tmo):
    i = pl.program_id(0)
    n_used = nu_ref[0]
    slot = i % 2

    def row_copy(tile, dst_slot, r):
        t = tok_ref[tile * tmo + r]
        return pltpu.make_async_copy(x_hbm.at[pl.ds(t, 1), :], xbuf.at[dst_slot, pl.ds(r, 1), :], sem.at[dst_slot])

    def experts(gather_next):
        pltpu.make_async_copy(x_hbm.at[pl.ds(0, tmo), :], xbuf.at[slot], sem.at[slot]).wait()
        x = xbuf[slot]
        xn = (x * lax.rsqrt(jnp.mean(x * x, axis=-1, keepdims=True) + EPS) * g_ref[...]).astype(BF16)
        if gather_next:
            for r in range(tmo):
                row_copy(i + 1, 1 - slot, r).start()
        hg = jnp.dot(xn, wg_ref[0], preferred_element_type=F32)
        hu = jnp.dot(xn, wu_ref[0], preferred_element_type=F32)
        h = (hg * jax.nn.sigmoid(hg) * hu).astype(BF16)
        y_ref[...] = _pack_bf16_halves(jnp.dot(h, wd_ref[0], preferred_element_type=F32))

    @pl.when(i == 0)
    def _():
        def issue(r, c):
            row_copy(0, 0, r).start()
            return c
        lax.fori_loop(0, tmo, issue, 0)

    @pl.when(i + 1 < n_used)
    def _():
        experts(True)

    @pl.when(i + 1 == n_used)
    def _():
        experts(False)

    @pl.when(i >= n_used)
    def _():
        y_ref[...] = jnp.zeros_like(y_ref)


def _moe(x, g, plan, wg, wu, wd, tmo):
    T, D = x.shape
    tok, tile_exp, n_used, _ = plan
    NP = tok.shape[0]
    E, _, DE = wg.shape
    grid_spec = pltpu.PrefetchScalarGridSpec(
        num_scalar_prefetch=3,
        grid=(NP // tmo,),
        in_specs=[
            pl.BlockSpec(memory_space=pl.ANY),
            pl.BlockSpec((1, D), lambda i, te, tk, nu: (0, 0)),
            pl.BlockSpec((1, D, DE), lambda i, te, tk, nu: (te[i], 0, 0)),
            pl.BlockSpec((1, D, DE), lambda i, te, tk, nu: (te[i], 0, 0)),
            pl.BlockSpec((1, DE, D), lambda i, te, tk, nu: (te[i], 0, 0)),
        ],
        out_specs=pl.BlockSpec((tmo, D // 2), lambda i, te, tk, nu: (i, 0)),
        scratch_shapes=[pltpu.VMEM((2, tmo, D), F32), pltpu.SemaphoreType.DMA((2,))],
    )
    return pl.pallas_call(
        functools.partial(_moe_kernel, tmo=tmo),
        grid_spec=grid_spec,
        out_shape=jax.ShapeDtypeStruct((NP, D // 2), jnp.uint32),
        compiler_params=_cparams("arbitrary"),
        name="moe_experts",
    )(tile_exp, tok, n_used, x, g[None], wg, wu, wd)


def _combine_kernel(pos_ref, x_ref, r_ref, y_hbm, out_ref, ybuf, sem, *, tc):
    i = pl.program_id(0)
    n = pl.num_programs(0)
    slot = i % 2

    def row_copy(tile, dst_slot, r, s):
        p = pos_ref[2 * (tile * tc + r) + s]
        return pltpu.make_async_copy(y_hbm.at[pl.ds(p, 1), :], ybuf.at[dst_slot, s, pl.ds(r, 1), :], sem.at[dst_slot])

    @pl.when(i == 0)
    def _():
        def issue(r, c):
            for s in range(2):
                row_copy(0, 0, r, s).start()
            return c
        lax.fori_loop(0, tc, issue, 0)

    @pl.when(i + 1 < n)
    def _():
        for r in range(tc):
            for s in range(2):
                row_copy(i + 1, 1 - slot, r, s).start()

    for s in range(2):
        pltpu.make_async_copy(y_hbm.at[pl.ds(0, tc), :], ybuf.at[slot, s], sem.at[slot]).wait()
    half = out_ref.shape[1] // 2
    lo0, hi0 = _unpack_bf16_halves(ybuf[slot, 0])
    lo1, hi1 = _unpack_bf16_halves(ybuf[slot, 1])
    w0 = r_ref[:, 2:3]
    w1 = r_ref[:, 3:4]
    out_ref[:, :half] = x_ref[:, :half] + (w0 * lo0 + w1 * lo1)
    out_ref[:, half:] = x_ref[:, half:] + (w0 * hi0 + w1 * hi1)


def _combine(x, route, y_sorted, pos):
    T, D = x.shape
    tc = _tile(T, 256)
    grid_spec = pltpu.PrefetchScalarGridSpec(
        num_scalar_prefetch=1,
        grid=(T // tc,),
        in_specs=[pl.BlockSpec((tc, D), lambda i, p: (i, 0)), pl.BlockSpec((tc, LANES), lambda i, p: (i, 0)),
                  pl.BlockSpec(memory_space=pl.ANY)],
        out_specs=pl.BlockSpec((tc, D), lambda i, p: (i, 0)),
        scratch_shapes=[pltpu.VMEM((2, 2, tc, D // 2), jnp.uint32), pltpu.SemaphoreType.DMA((2,))],
    )
    return pl.pallas_call(
        functools.partial(_combine_kernel, tc=tc),
        grid_spec=grid_spec,
        out_shape=jax.ShapeDtypeStruct((T, D), F32),
        compiler_params=_cparams("arbitrary"),
        name="moe_combine",
    )(pos, x, route, y_sorted)


def _layer_weights(p, H):
    D = p['w_in'].shape[0]
    C = p['w_dw'].shape[1]
    S = p['ssm_d'].shape[0]
    AW = H * p['q_gain'].shape[0]
    o_u = 2 * C
    o_q = o_u + S
    o_f = o_q + 3 * AW
    o_g = o_f + H
    perm = lambda a: jnp.concatenate([a[..., o_g:], a[..., o_q:o_f], a[..., :o_u], a[..., o_u:o_q]], axis=-1)
    cols = dict(g=0, q=3 * D, conv=3 * D + 3 * AW, u=3 * D + 3 * AW + 2 * C)
    wf = jnp.zeros((D, LANES), F32).at[:, :H].set(p['w_in'][:, o_f:o_g]).astype(BF16)
    bf = jnp.zeros((1, LANES), F32).at[0, :H].set(p['b_in'][o_f:o_g])
    return perm(p['w_in']).astype(BF16), perm(p['b_in'])[None], wf, bf, cols


def _mixer_and_moe(x3, p, conv_buf, ssm_h0, past, H, layer, depth, kv_prev):
    B, L, D = x3.shape
    T = B * L
    x = x3.reshape(T, D)
    hd = p['q_gain'].shape[0]
    AW = H * hd
    K, C = p['w_dw'].shape
    G, P, gw = p['ssm_b_re'].shape
    S = G * gw
    w_main, b_main, wf, bf, cols = _layer_weights(p, H)
    z, logf = _in_proj(x, p['g_mix'][None], w_main, b_main, wf, bf)

    buf0 = jnp.zeros((B, CONV_HALO, C), F32)
    if conv_buf is not None:
        buf0 = buf0.at[:, CONV_HALO - (K - 1):].set(conv_buf)
    ha, nc = _conv_branch(z, cols['conv'], B, L, buf0, p['w_dw'], p['b_dw'], p['ln_g'], p['ln_b'])
    new_conv = nc[:, CONV_HALO - (K - 1):]

    mats = _ssm_matrices(p['ssm_a_re'], p['ssm_a_im'], p['ssm_log_dt'], p['ssm_b_re'], p['ssm_b_im'],
                         p['ssm_c_re'], p['ssm_c_im'])
    h0 = jnp.zeros((B, G, P, 2), F32) if ssm_h0 is None else ssm_h0
    if (L // SSM_CHUNK) % 8 == 0:
        ys, new_ssm = _ssm_branch_natural(z, cols['u'], B, L, mats, h0)
    else:
        ys, new_ssm = _ssm_branch(z[:, cols['u']:cols['u'] + S], B, L, mats, h0)

    vcol = cols['q'] + 2 * AW
    qa, ka, k5, v5 = _qk_prep(z, cols['q'], B, L, logf, p['q_gain'], p['k_gain'], H, layer, depth, kv_prev)
    if past is None:
        tq = _tile(L, 256, LANES)
        o = _attention(qa, ka, z, vcol, B, L, H, hd, tq, 2 * tq if L % (2 * tq) == 0 else tq)
    else:
        nk = ((L + LANES - 1) // LANES) * LANES
        pad_rows = lambda a: jnp.pad(a.reshape(B, L, -1), ((0, 0), (0, nk - L), (0, 0))).reshape(B * nk, -1)
        o = _cached_attention(qa, pad_rows(ka), pad_rows(z[:, vcol:vcol + AW]), *past, layer, B, L, H, hd)

    bfw = lambda a: a.astype(BF16)
    x = _merge(x, ha, ys, z, cols['u'], cols['g'], o, p['ssm_d'], bfw(p['w_glu']), p['b_glu'],
               bfw(p['w_conv_out']), bfw(p['w_ssm_out']), bfw(p['w_attn_out']), bfw(p['w_out']))

    route = _route(x, p['g_ffn'], p['w_route_g'], p['b_route_g'], p['w_route_e'], p['b_route_e'])
    E = p['w_e_gate'].shape[0]
    tmo = _tile(T, 256 if 2 * T >= 256 * E else 64)
    plan = _dispatch_plan(route, E, tmo)
    y_sorted = _moe(x, p['g_ffn'], plan, bfw(p['w_e_gate']), bfw(p['w_e_up']), bfw(p['w_e_down']), tmo)
    x = _combine(x, route, y_sorted, plan[3])

    state = (new_conv, new_ssm, logf[:, :H].reshape(B, L, H))
    return x.reshape(B, L, D), state, (k5, v5)


def kernel(x_prompt, x_sample, cache_conv, state_ssm, cache_k, cache_v, cache_logf, g_mix, w_in, b_in, w_dw, b_dw, ln_g, ln_b, w_conv_out, ssm_a_re, ssm_a_im, ssm_log_dt, ssm_b_re, ssm_b_im, ssm_c_re, ssm_c_im, ssm_d, w_glu, b_glu, w_ssm_out, q_gain, k_gain, w_attn_out, w_out, g_ffn, w_route_g, b_route_g, w_route_e, b_route_e, w_e_gate, w_e_up, w_e_down):
    depth = w_in.shape[0]
    H = cache_logf.shape[-1]
    stacked = dict(g_mix=g_mix, w_in=w_in, b_in=b_in, w_dw=w_dw, b_dw=b_dw, ln_g=ln_g, ln_b=ln_b,
                   w_conv_out=w_conv_out, ssm_a_re=ssm_a_re, ssm_a_im=ssm_a_im, ssm_log_dt=ssm_log_dt,
                   ssm_b_re=ssm_b_re, ssm_b_im=ssm_b_im, ssm_c_re=ssm_c_re, ssm_c_im=ssm_c_im, ssm_d=ssm_d,
                   w_glu=w_glu, b_glu=b_glu, w_ssm_out=w_ssm_out, q_gain=q_gain, k_gain=k_gain,
                   w_attn_out=w_attn_out, w_out=w_out, g_ffn=g_ffn, w_route_g=w_route_g, b_route_g=b_route_g,
                   w_route_e=w_route_e, b_route_e=b_route_e, w_e_gate=w_e_gate, w_e_up=w_e_up, w_e_down=w_e_down)
    xp, xs = x_prompt, x_sample
    st_p, st_s = [], []
    kv_p = kv_s = None
    nb, PL, hd = cache_k.shape[1], cache_k.shape[2], cache_k.shape[4]
    past = (jnp.transpose(cache_k, (0, 1, 3, 4, 2)).reshape(depth, nb, H * hd, PL),
            jnp.transpose(cache_v, (0, 1, 3, 4, 2)).reshape(depth, nb, H * hd, PL),
            jnp.transpose(cache_logf, (0, 1, 3, 2)))
    for l in range(depth):
        p = {k: v[l] for k, v in stacked.items()}
        xp, st, kv_p = _mixer_and_moe(xp, p, None, None, None, H, l, depth, kv_p)
        st_p.append(st)
        xs, st, kv_s = _mixer_and_moe(xs, p, cache_conv[l], state_ssm[l], past, H, l, depth, kv_s)
        st_s.append(st)
    stack = lambda sts, i: jnp.stack([s[i] for s in sts])

    def kv_out(a):
        return a if a.ndim == 5 else jnp.transpose(a.reshape(a.shape[0], a.shape[1], H, hd, a.shape[3]), (0, 1, 4, 2, 3))

    return ((xp, xs) + (stack(st_p, 0), stack(st_p, 1), kv_out(kv_p[0]), kv_out(kv_p[1]), stack(st_p, 2))
            + (stack(st_s, 0), stack(st_s, 1), kv_out(kv_s[0]), kv_out(kv_s[1]), stack(st_s, 2)))
```

```python
import functools
import math

import numpy as np
import jax
import jax.numpy as jnp
from jax import lax
from jax.experimental import pallas as pl
from jax.experimental.pallas import tpu as pltpu

F32 = jnp.float32
BF16 = jnp.bfloat16
EPS = 1e-6
NEG = -1e30
LANES = 128
CONV_HALO = 32
SSM_CHUNK = 16
MXU_DIM = 256
BIAS_SPLIT = 3
VMEM_LIMIT_BYTES = 56 * 1024 * 1024


def _cparams(*sem):
    return pltpu.CompilerParams(dimension_semantics=sem, vmem_limit_bytes=VMEM_LIMIT_BYTES)


def _tile(n, pref, mult=8):
    if n <= pref:
        return n
    t = (pref // mult) * mult
    while t >= mult:
        if n % t == 0:
            return t
        t -= mult
    return n


def _const_spec(shape):
    nd = len(shape)
    return pl.BlockSpec(shape, lambda *_: (0,) * nd)


def _in_proj_kernel(x_ref, g_ref, w_ref, b_ref, wf_ref, bf_ref, z_ref, lf_ref, xn_ref):
    @pl.when(pl.program_id(1) == 0)
    def _():
        x = x_ref[...]
        r = lax.rsqrt(jnp.mean(x * x, axis=-1, keepdims=True) + EPS)
        xn = (x * r * g_ref[...]).astype(BF16)
        xn_ref[...] = xn
        zf = jnp.dot(xn, wf_ref[...], preferred_element_type=F32) + bf_ref[...]
        lf_ref[...] = jnp.minimum(zf, 0.0) - jnp.log(1.0 + jnp.exp(-jnp.abs(zf)))

    z = jnp.dot(xn_ref[...], w_ref[...], preferred_element_type=F32) + b_ref[...]
    z_ref[...] = z.astype(z_ref.dtype)


def _in_proj(x, g, w, b, wf, bf):
    T, D = x.shape
    N = w.shape[1]
    tm = _tile(T, 1024)
    tn = _tile(N, 1536, LANES)
    return pl.pallas_call(
        _in_proj_kernel,
        grid=(T // tm, N // tn),
        in_specs=[
            pl.BlockSpec((tm, D), lambda i, j: (i, 0)),
            pl.BlockSpec((1, D), lambda i, j: (0, 0)),
            pl.BlockSpec((D, tn), lambda i, j: (0, j)),
            pl.BlockSpec((1, tn), lambda i, j: (0, j)),
            pl.BlockSpec((D, LANES), lambda i, j: (0, 0)),
            pl.BlockSpec((1, LANES), lambda i, j: (0, 0)),
        ],
        out_specs=[
            pl.BlockSpec((tm, tn), lambda i, j: (i, j)),
            pl.BlockSpec((tm, LANES), lambda i, j: (i, 0)),
        ],
        out_shape=[jax.ShapeDtypeStruct((T, N), BF16), jax.ShapeDtypeStruct((T, LANES), F32)],
        scratch_shapes=[pltpu.VMEM((tm, D), BF16)],
        compiler_params=_cparams("parallel", "arbitrary"),
        name="in_proj",
    )(x, g, w, b, wf, bf)


def _conv_kernel(a_ref, b_ref, buf0_ref, w_ref, bdw_ref, lng_ref, lnb_ref, h_ref, nc_ref, u_ref, acc_ref,
                 *, tt, C, K):
    ti = pl.program_id(1)

    @pl.when(ti == 0)
    def _():
        u_ref[0:CONV_HALO, :] = buf0_ref[0]

    @pl.when(ti > 0)
    def _():
        u_ref[0:CONV_HALO, :] = u_ref[tt:tt + CONV_HALO, :]

    u_ref[CONV_HALO:CONV_HALO + tt, :] = a_ref[...].astype(F32) * jax.nn.sigmoid(b_ref[...].astype(F32))
    base = CONV_HALO - (K - 1)
    rc = min(tt, 128)
    for r0 in range(0, tt, rc):
        for c0 in range(0, C, LANES):
            acc = jnp.zeros((rc, LANES), F32)
            for j in range(K):
                acc = acc + w_ref[j:j + 1, c0:c0 + LANES] * u_ref[r0 + base + j:r0 + base + j + rc, c0:c0 + LANES]
            acc_ref[r0:r0 + rc, c0:c0 + LANES] = acc
    h = acc_ref[...] + bdw_ref[...]
    mu = jnp.mean(h, axis=-1, keepdims=True)
    d = h - mu
    var = jnp.mean(d * d, axis=-1, keepdims=True)
    y = d * lax.rsqrt(var + EPS) * lng_ref[...] + lnb_ref[...]
    h_ref[...] = (y * jax.nn.sigmoid(y)).astype(h_ref.dtype)

    @pl.when(ti == pl.num_programs(1) - 1)
    def _():
        nc_ref[0] = u_ref[tt:tt + CONV_HALO, :]


def _conv_branch(z, col0, B, L, buf0, w_dw, b_dw, ln_g, ln_b):
    K, C = w_dw.shape
    assert K - 1 <= CONV_HALO and L >= CONV_HALO and C % LANES == 0 and col0 % C == 0
    tt = _tile(L, 256)
    nt = L // tt
    cb = col0 // C
    wpad = jnp.zeros((CONV_HALO, C), F32).at[:K].set(w_dw)
    kern = functools.partial(_conv_kernel, tt=tt, C=C, K=K)
    return pl.pallas_call(
        kern,
        grid=(B, nt),
        in_specs=[
            pl.BlockSpec((tt, C), lambda b, t: (b * nt + t, cb)),
            pl.BlockSpec((tt, C), lambda b, t: (b * nt + t, cb + 1)),
            pl.BlockSpec((1, CONV_HALO, C), lambda b, t: (b, 0, 0)),
            _const_spec((CONV_HALO, C)),
            _const_spec((1, C)), _const_spec((1, C)), _const_spec((1, C)),
        ],
        out_specs=[
            pl.BlockSpec((tt, C), lambda b, t: (b * nt + t, 0)),
            pl.BlockSpec((1, CONV_HALO, C), lambda b, t: (b, 0, 0)),
        ],
        out_shape=[jax.ShapeDtypeStruct((B * L, C), BF16), jax.ShapeDtypeStruct((B, CONV_HALO, C), F32)],
        scratch_shapes=[pltpu.VMEM((CONV_HALO + tt, C), F32), pltpu.VMEM((tt, C), F32)],
        compiler_params=_cparams("parallel", "arbitrary"),
        name="conv_branch",
    )(z, z, buf0, wpad, b_dw[None], ln_g[None], ln_b[None])


def _ssm_matrices(a_re, a_im, log_dt, b_re, b_im, c_re, c_im):
    hp = lax.Precision.HIGHEST
    G, P, gw = b_re.shape
    S = SSM_CHUNK
    lam_re = jnp.minimum(a_re, -1e-4)
    lam_im = a_im
    dt = jnp.exp(log_dt)[:, None]
    d = jnp.arange(S + 1, dtype=F32)[None, :, None]
    mag = jnp.exp(lam_re[:, None, :] * dt[:, None, :] * d)
    ang = lam_im[:, None, :] * dt[:, None, :] * d
    pw_re, pw_im = mag * jnp.cos(ang), mag * jnp.sin(ang)
    den = lam_re * lam_re + lam_im * lam_im
    xr, xi = pw_re[:, 1] - 1.0, pw_im[:, 1]
    cf_re = (xr * lam_re + xi * lam_im) / den
    cf_im = (xi * lam_re - xr * lam_im) / den
    bb_re = cf_re[..., None] * b_re - cf_im[..., None] * b_im
    bb_im = cf_re[..., None] * b_im + cf_im[..., None] * b_re
    cp_re = c_re[:, None] * pw_re[:, :, None, :] - c_im[:, None] * pw_im[:, :, None, :]
    cp_im = c_re[:, None] * pw_im[:, :, None, :] + c_im[:, None] * pw_re[:, :, None, :]
    kk = (jnp.einsum('gdcp,gpe->gdce', cp_re[:, :S], bb_re, precision=hp)
          - jnp.einsum('gdcp,gpe->gdce', cp_im[:, :S], bb_im, precision=hp))
    s_idx = jnp.arange(S)[:, None]
    t_idx = jnp.arange(S)[None, :]
    lag = jnp.clip(t_idx - s_idx, 0, S - 1)
    m = kk[:, lag] * (t_idx >= s_idx)[None, :, :, None, None]
    m = m.transpose(0, 1, 4, 2, 3).reshape(G, S * gw, S * gw)
    rev_re = pw_re[:, S - 1 - jnp.arange(S)]
    rev_im = pw_im[:, S - 1 - jnp.arange(S)]
    p_re = rev_re[:, :, None, :] * bb_re.transpose(0, 2, 1)[:, None] - rev_im[:, :, None, :] * bb_im.transpose(0, 2, 1)[:, None]
    p_im = rev_re[:, :, None, :] * bb_im.transpose(0, 2, 1)[:, None] + rev_im[:, :, None, :] * bb_re.transpose(0, 2, 1)[:, None]
    p_re = p_re.reshape(G, S * gw, P)
    p_im = p_im.reshape(G, S * gw, P)
    w1 = jnp.concatenate([m, p_re, p_im, p_im, p_re], axis=-1)
    q_re = cp_re[:, 1:].transpose(0, 3, 1, 2).reshape(G, P, S * gw)
    q_im = -cp_im[:, 1:].transpose(0, 3, 1, 2).reshape(G, P, S * gw)
    q = jnp.concatenate([q_re, q_im], axis=1)
    ar, ai = pw_re[:, S], pw_im[:, S]
    a = jnp.zeros((G, 8, 2 * P), F32)
    a = a.at[:, 0].set(jnp.concatenate([ar, ar], -1))
    a = a.at[:, 1].set(jnp.concatenate([-ai, ai], -1))
    a = a.at[:, 2].set(jnp.concatenate([ai, -ai], -1))
    return w1.astype(BF16), q.astype(BF16), a


def _ssm_kernel(u_ref, w1_ref, q_ref, a_ref, h0_ref, h0s_ref, y_ref, hf_ref, tmp_ref, hh_ref, *, nk, B, CW, P2):
    tmp_ref[...] = jnp.dot(u_ref[0], w1_ref[0], preferred_element_type=F32)
    a1 = a_ref[0, 0:1, :]
    a2 = a_ref[0, 1:2, :]
    a2s = a_ref[0, 2:3, :]

    def body(k, carry):
        h, hs = carry
        r = pl.multiple_of(k * B, B)
        hh_ref[pl.ds(r, B), :] = h
        inc = tmp_ref[pl.ds(r, B), CW:CW + P2]
        incs = tmp_ref[pl.ds(r, B), CW + P2:CW + 2 * P2]
        return a1 * h + a2 * hs + inc, a1 * hs + a2s * h + incs

    h, _ = lax.fori_loop(0, nk, body, (h0_ref[0], h0s_ref[0]))
    hf_ref[0] = h
    y = tmp_ref[:, 0:CW] + jnp.dot(hh_ref[...].astype(BF16), q_ref[0], preferred_element_type=F32)
    y_ref[0] = y.astype(y_ref.dtype)


def _ssm_branch(u, B, L, mats, h0):
    w1, q, a = mats
    G, CW, _ = w1.shape
    P2 = q.shape[1]
    gw = CW // SSM_CHUNK
    assert CW == MXU_DIM and L % SSM_CHUNK == 0
    nk = L // SSM_CHUNK
    R = nk * B
    ug = u.reshape(B, nk, SSM_CHUNK, G, gw).transpose(3, 1, 0, 2, 4).reshape(G, R, CW)
    h0r = jnp.concatenate([h0[..., 0], h0[..., 1]], -1).transpose(1, 0, 2)
    h0s = jnp.concatenate([h0[..., 1], h0[..., 0]], -1).transpose(1, 0, 2)
    kern = functools.partial(_ssm_kernel, nk=nk, B=B, CW=CW, P2=P2)
    y, hf = pl.pallas_call(
        kern,
        grid=(G,),
        in_specs=[
            pl.BlockSpec((1, R, CW), lambda g: (g, 0, 0)),
            pl.BlockSpec((1, CW, CW + 2 * P2), lambda g: (g, 0, 0)),
            pl.BlockSpec((1, P2, CW), lambda g: (g, 0, 0)),
            pl.BlockSpec((1, 8, P2), lambda g: (g, 0, 0)),
            pl.BlockSpec((1, B, P2), lambda g: (g, 0, 0)),
            pl.BlockSpec((1, B, P2), lambda g: (g, 0, 0)),
        ],
        out_specs=[
            pl.BlockSpec((1, R, CW), lambda g: (g, 0, 0)),
            pl.BlockSpec((1, B, P2), lambda g: (g, 0, 0)),
        ],
        out_shape=[jax.ShapeDtypeStruct((G, R, CW), BF16), jax.ShapeDtypeStruct((G, B, P2), F32)],
        scratch_shapes=[pltpu.VMEM((R, CW + 2 * P2), F32), pltpu.VMEM((R, P2), F32)],
        compiler_params=_cparams("parallel"),
        name="ssm_branch",
    )(ug, w1, q, a, h0r, h0s)
    y = y.reshape(G, nk, B, SSM_CHUNK, gw).transpose(2, 1, 3, 0, 4).reshape(B * L, G * gw)
    P = P2 // 2
    new_state = jnp.stack([hf[..., :P], hf[..., P:]], axis=-1).transpose(1, 0, 2, 3)
    return y, new_state


def _ssm_seq_kernel(u_ref, w1_ref, q_ref, a_ref, h0_ref, h0s_ref, y_ref, hf_ref, uf_ref, tmp_ref, hh_ref, yg_ref,
                    *, nk, G, gw, CW, P2):
    S = SSM_CHUNK
    per = LANES // gw
    lane = lax.broadcasted_iota(jnp.int32, (1, LANES), 1)
    slot = [(lane >= i * gw) & (lane < (i + 1) * gw) for i in range(per)]
    for j in range(G // per):
        uf_ref[j] = u_ref[:, j * LANES:(j + 1) * LANES].astype(F32)

    for j in range(G // per):
        xs = [uf_ref[j, pl.ds(s, nk, stride=S), :] for s in range(S)]
        for g8 in range(per):
            g = j * per + g8
            halves = []
            for d in range(S // per):
                acc = jnp.zeros((nk, LANES), F32)
                for s8 in range(per):
                    x = xs[d * per + s8]
                    sh = ((s8 - g8) * gw) % LANES
                    acc = jnp.where(slot[s8], pltpu.roll(x, sh, axis=1) if sh else x, acc)
                halves.append(acc)
            ug = jnp.concatenate(halves, axis=1).astype(BF16)
            tmp_ref[:, g, :] = jnp.dot(ug, w1_ref[g], preferred_element_type=F32)

    a1, a2, a2s = a_ref[0], a_ref[1], a_ref[2]

    def body(k, carry):
        h, hs = carry
        hh_ref[k] = h
        inc = tmp_ref[k, :, CW:CW + P2]
        incs = tmp_ref[k, :, CW + P2:CW + 2 * P2]
        return a1 * h + a2 * hs + inc, a1 * hs + a2s * h + incs

    h, _ = lax.fori_loop(0, nk, body, (h0_ref[0], h0s_ref[0]))
    hf_ref[0] = h

    for g in range(G):
        yg_ref[g] = tmp_ref[:, g, 0:CW] + jnp.dot(hh_ref[:, g, :].astype(BF16), q_ref[g], preferred_element_type=F32)

    for j in range(G // per):
        for t in range(S):
            d, t8 = divmod(t, per)
            acc = jnp.zeros((nk, LANES), F32)
            for g8 in range(per):
                x = yg_ref[j * per + g8, :, d * LANES:(d + 1) * LANES]
                sh = ((g8 - t8) * gw) % LANES
                acc = jnp.where(slot[g8], pltpu.roll(x, sh, axis=1) if sh else x, acc)
            uf_ref[j, pl.ds(t, nk, stride=S), :] = acc
    for j in range(G // per):
        y_ref[:, j * LANES:(j + 1) * LANES] = uf_ref[j].astype(y_ref.dtype)


def _ssm_branch_natural(z, ucol, B, L, mats, h0):
    w1, q, a = mats
    G, CW, _ = w1.shape
    P2 = q.shape[1]
    gw = CW // SSM_CHUNK
    SW = G * gw
    nk = L // SSM_CHUNK
    assert CW == MXU_DIM and L % SSM_CHUNK == 0 and nk % 8 == 0 and LANES % gw == 0 and SW % LANES == 0
    assert SSM_CHUNK % (LANES // gw) == 0 and ucol % SW == 0
    h0r = jnp.concatenate([h0[..., 0], h0[..., 1]], -1)
    h0s = jnp.concatenate([h0[..., 1], h0[..., 0]], -1)
    a3 = a[:, :3].transpose(1, 0, 2)
    kern = functools.partial(_ssm_seq_kernel, nk=nk, G=G, gw=gw, CW=CW, P2=P2)
    single = pl.Buffered(1)
    y, hf = pl.pallas_call(
        kern,
        grid=(B,),
        in_specs=[
            pl.BlockSpec((L, SW), lambda b: (b, ucol // SW)),
            pl.BlockSpec(w1.shape, lambda b: (0, 0, 0), pipeline_mode=single),
            pl.BlockSpec(q.shape, lambda b: (0, 0, 0), pipeline_mode=single),
            pl.BlockSpec(a3.shape, lambda b: (0, 0, 0)),
            pl.BlockSpec((1, G, P2), lambda b: (b, 0, 0)),
            pl.BlockSpec((1, G, P2), lambda b: (b, 0, 0)),
        ],
        out_specs=[
            pl.BlockSpec((L, SW), lambda b: (b, 0)),
            pl.BlockSpec((1, G, P2), lambda b: (b, 0, 0)),
        ],
        out_shape=[jax.ShapeDtypeStruct((B * L, SW), BF16), jax.ShapeDtypeStruct((B, G, P2), F32)],
        scratch_shapes=[pltpu.VMEM((SW // LANES, L, LANES), F32), pltpu.VMEM((nk, G, CW + 2 * P2), F32),
                        pltpu.VMEM((nk, G, P2), F32), pltpu.VMEM((G, nk, CW), F32)],
        compiler_params=_cparams("parallel"),
        name="ssm_branch_seq",
    )(z, w1, q, a3, h0r, h0s)
    P = P2 // 2
    return y, jnp.stack([hf[..., :P], hf[..., P:]], axis=-1)


def _bias_table(H):
    assert H % 2 == 0 and BIAS_SPLIT * H <= LANES
    pk = np.zeros((LANES, (H // 2) * LANES), np.float32)
    for h in range(H):
        p, e = divmod(h, 2)
        for i in range(BIAS_SPLIT):
            pk[i * H + h, p * LANES + e * BIAS_SPLIT + i] = -1.0
    return jnp.asarray(pk, BF16)


def _stack_pair_queries(q, e, hd):
    lane = lax.broadcasted_iota(jnp.int32, (1, LANES), 1)
    qe = jnp.where((lane >= e * hd) & (lane < (e + 1) * hd), q, jnp.zeros_like(q))
    one = jnp.where((lane >= e * BIAS_SPLIT) & (lane < (e + 1) * BIAS_SPLIT), 1.0, 0.0).astype(q.dtype)
    return jnp.concatenate([qe, jnp.broadcast_to(one, q.shape)], axis=1)


def _split3(x):
    hi = x.astype(BF16)
    r1 = x - hi.astype(F32)
    mid = r1.astype(BF16)
    lo = (r1 - mid.astype(F32)).astype(BF16)
    return hi, mid, lo


def _forget_cumsum(lf, carry, H):
    tl = lf.shape[0]
    lane = lax.broadcasted_iota(jnp.int32, (1, LANES), 1)
    lf = jnp.where(lane < H, lf, 0.0)
    tri = (lax.broadcasted_iota(jnp.int32, (tl, tl), 0) >= lax.broadcasted_iota(jnp.int32, (tl, tl), 1)).astype(BF16)
    hi, mid, lo = _split3(lf)
    c = (jnp.dot(tri, hi, preferred_element_type=F32) + jnp.dot(tri, mid, preferred_element_type=F32)
         + jnp.dot(tri, lo, preferred_element_type=F32))
    return c + carry


def _pack_split(c, H):
    hi, mid, lo = _split3(c)
    packed = hi.astype(F32) + pltpu.roll(mid.astype(F32), H, axis=1) + pltpu.roll(lo.astype(F32), 2 * H, axis=1)
    return packed.astype(BF16)


def _pair_rmsnorm(x, gain, lo_half, hd):
    sq = x * x
    s_lo = jnp.sum(jnp.where(lo_half, sq, 0.0), axis=-1, keepdims=True)
    s_hi = jnp.sum(jnp.where(lo_half, 0.0, sq), axis=-1, keepdims=True)
    ms = jnp.where(lo_half, s_lo, s_hi) * (1.0 / hd)
    return x * lax.rsqrt(ms + EPS) * gain


def _qk_kernel(*refs, tl, H, hd, scale, aliased, time_minor, own):
    zq_ref, zk_ref, zv_ref, lf_ref, qg_ref, kg_ref, pk_ref = refs[:7]
    qa_ref, ka_ref, k5_ref, v5_ref, c_ref = refs[9:] if aliased else refs[7:]

    @pl.when(pl.program_id(1) == 0)
    def _():
        c_ref[...] = jnp.zeros_like(c_ref)

    for other in range(k5_ref.shape[0]):
        if other != own:
            k5_ref[other] = jnp.zeros(k5_ref.shape[1:], F32)
            v5_ref[other] = jnp.zeros(v5_ref.shape[1:], F32)

    c = _forget_cumsum(lf_ref[...], c_ref[0:1, :], H)
    c_ref[...] = jnp.broadcast_to(c[tl - 1:tl, :], c_ref.shape)
    bk = jnp.dot(_pack_split(c, H), pk_ref[...], preferred_element_type=F32)
    lo_half = lax.broadcasted_iota(jnp.int32, (1, LANES), 1) < hd
    for p in range(H // 2):
        sl = slice(p * LANES, (p + 1) * LANES)
        qn = _pair_rmsnorm(zq_ref[:, sl].astype(F32), qg_ref[:, sl], lo_half, hd) * scale
        kn = _pair_rmsnorm(zk_ref[:, sl].astype(F32), kg_ref[:, sl], lo_half, hd)
        vv = zv_ref[:, sl].astype(F32)
        qa_ref[:, sl] = qn.astype(BF16)
        ka_ref[:, 2 * p * LANES:(2 * p + 1) * LANES] = kn.astype(BF16)
        ka_ref[:, (2 * p + 1) * LANES:(2 * p + 2) * LANES] = bk[:, sl].astype(BF16)
        if time_minor:
            k5_ref[own, 0, sl, :] = kn.T
            v5_ref[own, 0, sl, :] = vv.T
        else:
            for e in range(2):
                k5_ref[own, 0, :, 2 * p + e, :] = kn[:, e * hd:(e + 1) * hd]
                v5_ref[own, 0, :, 2 * p + e, :] = vv[:, e * hd:(e + 1) * hd]


def _qk_prep(z, qcol, B, L, logf, q_gain, k_gain, H, layer, depth, kv_prev):
    hd = q_gain.shape[0]
    AW = H * hd
    assert 2 * hd == LANES and qcol % AW == 0
    tl = _tile(L, 256)
    nt = L // tl
    qb = qcol // AW
    pk = _bias_table(H)
    NA = H * LANES
    aliased = kv_prev is not None
    time_minor = tl % LANES == 0
    nl, l0 = (1, layer) if aliased else (depth, 0)
    kern = functools.partial(_qk_kernel, tl=tl, H=H, hd=hd, scale=hd ** -0.5, aliased=aliased,
                             time_minor=time_minor, own=layer - l0)
    row = lambda b, t: b * nt + t
    in_specs = [
        pl.BlockSpec((tl, AW), lambda b, t: (row(b, t), qb)),
        pl.BlockSpec((tl, AW), lambda b, t: (row(b, t), qb + 1)),
        pl.BlockSpec((tl, AW), lambda b, t: (row(b, t), qb + 2)),
        pl.BlockSpec((tl, LANES), lambda b, t: (row(b, t), 0)),
        _const_spec((1, AW)), _const_spec((1, AW)), _const_spec(pk.shape),
    ]
    args = [z, z, z, logf, jnp.tile(q_gain, H)[None], jnp.tile(k_gain, H)[None], pk]
    if aliased:
        in_specs += [pl.BlockSpec(memory_space=pl.ANY), pl.BlockSpec(memory_space=pl.ANY)]
        args += list(kv_prev)
    if time_minor:
        kv_spec = pl.BlockSpec((nl, 1, AW, tl), lambda b, t: (l0, b, 0, t))
        kv_shape = jax.ShapeDtypeStruct((depth, B, AW, L), F32)
    else:
        kv_spec = pl.BlockSpec((nl, 1, tl, H, hd), lambda b, t: (l0, b, t, 0, 0))
        kv_shape = jax.ShapeDtypeStruct((depth, B, L, H, hd), F32)
    return pl.pallas_call(
        kern,
        grid=(B, nt),
        in_specs=in_specs,
        out_specs=[
            pl.BlockSpec((tl, AW), lambda b, t: (row(b, t), 0)),
            pl.BlockSpec((tl, NA), lambda b, t: (row(b, t), 0)),
            kv_spec, kv_spec,
        ],
        out_shape=[jax.ShapeDtypeStruct((B * L, AW), BF16), jax.ShapeDtypeStruct((B * L, NA), BF16),
                   kv_shape, kv_shape],
        scratch_shapes=[pltpu.VMEM((8, LANES), F32)],
        input_output_aliases={7: 2, 8: 3} if aliased else {},
        compiler_params=_cparams("parallel", "arbitrary"),
        name="qk_prep",
    )(*args)


def _attn_kernel(q_ref, k_ref, v_ref, o_ref, q2_ref, acc_ref, m_ref, *, tq, tk, hd):
    qi = pl.program_id(2)
    npp = q2_ref.shape[0]
    for pp in range(npp):
        q = q_ref[:, pp * LANES:(pp + 1) * LANES]
        for e in range(2):
            q2_ref[pp, e * tq:(e + 1) * tq, :] = _stack_pair_queries(q, e, hd)
    q_lo = pl.multiple_of(qi * tq, tq)
    m_ref[...] = jnp.full(m_ref.shape, NEG, F32)
    acc_ref[...] = jnp.zeros(acc_ref.shape, F32)

    def step(k0, width, masked):
        mask = None
        if masked:
            row = lax.broadcasted_iota(jnp.int32, (2 * tq, 1), 0)
            qpos = q_lo + jnp.where(row >= tq, row - tq, row)
            mask = k0 + lax.broadcasted_iota(jnp.int32, (1, width), 1) <= qpos
        ones = jnp.ones((width, LANES), BF16)
        chains = [(q2_ref[pp], k_ref[pl.ds(k0, width), 2 * pp * LANES:(2 * pp + 2) * LANES],
                   jnp.concatenate([v_ref[pl.ds(k0, width), pp * LANES:(pp + 1) * LANES], ones], axis=1),
                   m_ref.at[pp], acc_ref.at[pp]) for pp in range(npp)]
        _softmax_step(chains, mask)

    def full_body(j, c):
        step(pl.multiple_of(j * tk, tk), tk, False)
        return c

    n_wide = q_lo // tk
    lax.fori_loop(0, n_wide, full_body, 0)
    if tk != tq:
        @pl.when(q_lo - n_wide * tk >= tq)
        def _():
            step(pl.multiple_of(n_wide * tk, tq), tq, False)
    step(q_lo, tq, True)
    for pp in range(npp):
        o_ref[:, pp * LANES:(pp + 1) * LANES] = _pair_output(acc_ref[pp], tq, hd).astype(o_ref.dtype)


def _softmax_step(chains, mask):
    nt_dims = (((1,), (1,)), ((), ()))
    scores = [lax.dot_general(q2, k_aug, nt_dims, preferred_element_type=F32) for q2, k_aug, _, _, _ in chains]
    probs = []
    for s, (_, k_aug, _, m_ref, _) in zip(scores, chains):
        if mask is not None:
            s = jnp.where(mask, s, NEG)
        m_prev = m_ref[...]
        m_next = jnp.maximum(m_prev, jnp.max(s, axis=-1, keepdims=True))
        alpha = jnp.exp(m_prev - m_next)
        p = jnp.exp((s - jnp.concatenate([m_next] * (k_aug.shape[0] // LANES), axis=1)).astype(BF16))
        m_ref[...] = m_next
        probs.append((p, alpha))
    for (p, alpha), (_, _, v_aug, _, acc_ref) in zip(probs, chains):
        pv = jnp.dot(p, v_aug, preferred_element_type=F32)
        acc_ref[...] = jnp.concatenate([alpha, alpha], axis=1) * acc_ref[...] + pv


def _pair_output(acc, tq, hd):
    o = acc[:, 0:LANES] / acc[:, LANES:2 * LANES]
    lane = lax.broadcasted_iota(jnp.int32, (1, LANES), 1)
    return jnp.where(lane < hd, o[0:tq], o[tq:2 * tq])


def _attention(qa, ka, v, vcol, B, L, H, hd, tq, tk):
    npair = H // 2
    npp = 4 if npair % 4 == 0 else (2 if npair % 2 == 0 else 1)
    nq = L // tq
    assert L % tq == 0 and tk in (tq, 2 * tq) and tq % LANES == 0 and vcol % (npp * LANES) == 0
    vb = vcol // (npp * LANES)
    kern = functools.partial(_attn_kernel, tq=tq, tk=tk, hd=hd)
    return pl.pallas_call(
        kern,
        grid=(B, npair // npp, nq),
        in_specs=[
            pl.BlockSpec((tq, npp * LANES), lambda b, p, i: (b * nq + i, p)),
            pl.BlockSpec((L, npp * 2 * LANES), lambda b, p, i: (b, p)),
            pl.BlockSpec((L, npp * LANES), lambda b, p, i: (b, vb + p)),
        ],
        out_specs=pl.BlockSpec((tq, npp * LANES), lambda b, p, i: (b * nq + i, p)),
        out_shape=jax.ShapeDtypeStruct((B * L, H * hd), BF16),
        scratch_shapes=[pltpu.VMEM((npp, 2 * tq, 2 * LANES), BF16), pltpu.VMEM((npp, 2 * tq, 2 * LANES), F32),
                        pltpu.VMEM((npp, 2 * tq, LANES), F32)],
        compiler_params=_cparams("parallel", "parallel", "arbitrary"),
        name="fox_attention",
    )(qa, ka, v)


def _cached_attn_kernel(q_ref, kc_ref, vc_ref, lf_ref, kn_ref, vn_ref, o_ref,
                        q2_ref, acc_ref, m_ref, l_ref, c_ref, *, Lq, tl, H, hd):
    t = pl.program_id(1)
    npair = H // 2
    row = lax.broadcasted_iota(jnp.int32, (2 * Lq, 1), 0)

    @pl.when(t == 0)
    def _():
        c_ref[...] = jnp.zeros_like(c_ref)
        m_ref[...] = jnp.full(m_ref.shape, NEG, F32)
        l_ref[...] = jnp.zeros(l_ref.shape, F32)
        acc_ref[...] = jnp.zeros(acc_ref.shape, F32)
        for p in range(npair):
            q = q_ref[:, p * LANES:(p + 1) * LANES]
            for e in range(2):
                q2_ref[p, e * Lq:(e + 1) * Lq, :] = _stack_pair_queries(q, e, hd)

    def update(p, s, weigh):
        m_prev = m_ref[p]
        m_next = jnp.maximum(m_prev, jnp.max(s, axis=-1, keepdims=True))
        alpha = jnp.exp(m_prev - m_next)
        pr = jnp.exp(s - jnp.concatenate([m_next] * (s.shape[1] // LANES), axis=1))
        l_ref[p] = alpha * l_ref[p] + jnp.sum(pr, axis=-1, keepdims=True)
        acc_ref[p] = alpha * acc_ref[p] + weigh(pr.astype(BF16))
        m_ref[p] = m_next

    lf = lf_ref[0, 0]
    after = (lax.broadcasted_iota(jnp.int32, (tl, tl), 0) > lax.broadcasted_iota(jnp.int32, (tl, tl), 1)).astype(BF16)
    later = sum(jnp.dot(piece, after, preferred_element_type=F32) for piece in _split3(lf))
    later = later + jnp.concatenate([c_ref[...]] * (tl // LANES), axis=1)
    c_ref[...] = c_ref[...] + jnp.sum(lf, axis=-1, keepdims=True)
    nt_dims = (((1,), (1,)), ((), ()))
    for p in range(npair):
        sl = slice(p * LANES, (p + 1) * LANES)
        kp = kc_ref[0, 0, sl, :].astype(BF16)
        vp = vc_ref[0, 0, sl, :].astype(BF16)
        s = jnp.dot(q2_ref[p, :, 0:LANES], kp, preferred_element_type=F32)
        s = s + jnp.where(row < Lq, later[2 * p:2 * p + 1, :], later[2 * p + 1:2 * p + 2, :])
        update(p, s, lambda pr: lax.dot_general(pr, vp, nt_dims, preferred_element_type=F32))

    @pl.when(t == pl.num_programs(1) - 1)
    def _():
        nk = kn_ref.shape[0]
        mask = lax.broadcasted_iota(jnp.int32, (1, nk), 1) <= jnp.where(row >= Lq, row - Lq, row)
        lane = lax.broadcasted_iota(jnp.int32, (1, LANES), 1)
        for p in range(npair):
            s = lax.dot_general(q2_ref[p], kn_ref[:, 2 * p * LANES:(2 * p + 2) * LANES], nt_dims,
                                preferred_element_type=F32)
            vn = vn_ref[:, p * LANES:(p + 1) * LANES]
            update(p, jnp.where(mask, s, NEG), lambda pr: jnp.dot(pr, vn, preferred_element_type=F32))
            o = acc_ref[p] / l_ref[p]
            o_ref[:, p * LANES:(p + 1) * LANES] = jnp.where(lane < hd, o[0:Lq], o[Lq:2 * Lq]).astype(o_ref.dtype)


def _cached_attention(qa, ka_new, v_new, cache_kt, cache_vt, cache_lft, layer, B, Lq, H, hd):
    PL = cache_kt.shape[-1]
    AW = H * hd
    NA = H * LANES
    nk = ka_new.shape[0] // B
    tl = _tile(PL, 1024, LANES)
    nt = PL // tl
    npair = H // 2
    kern = functools.partial(_cached_attn_kernel, Lq=Lq, tl=tl, H=H, hd=hd)
    kv_spec = pl.BlockSpec((1, 1, AW, tl), lambda b, t: (layer, b, 0, nt - 1 - t))
    return pl.pallas_call(
        kern,
        grid=(B, nt),
        in_specs=[
            pl.BlockSpec((Lq, AW), lambda b, t: (b, 0)),
            kv_spec, kv_spec,
            pl.BlockSpec((1, 1, H, tl), lambda b, t: (layer, b, 0, nt - 1 - t)),
            pl.BlockSpec((nk, NA), lambda b, t: (b, 0)),
            pl.BlockSpec((nk, AW), lambda b, t: (b, 0)),
        ],
        out_specs=pl.BlockSpec((Lq, AW), lambda b, t: (b, 0)),
        out_shape=jax.ShapeDtypeStruct((B * Lq, AW), BF16),
        scratch_shapes=[pltpu.VMEM((npair, 2 * Lq, 2 * LANES), BF16), pltpu.VMEM((npair, 2 * Lq, LANES), F32),
                        pltpu.VMEM((npair, 2 * Lq, LANES), F32), pltpu.VMEM((npair, 2 * Lq, LANES), F32),
                        pltpu.VMEM((H, LANES), F32)],
        compiler_params=_cparams("parallel", "arbitrary"),
        name="fox_cached_attention",
    )(qa, cache_kt, cache_vt, cache_lft, ka_new, v_new)


def _merge_kernel(x_ref, ha_ref, ys_ref, u_ref, o_ref, ga_ref, gb_ref, gc_ref, d_ref, wglu_ref, bglu_ref,
                  wa_ref, wb_ref, wc_ref, wo_ref, out_ref, slab_ref):
    ys = ys_ref[...].astype(F32) + d_ref[...] * u_ref[...].astype(F32)
    hs = 0.5 * ys * (1.0 + jnp.tanh(math.sqrt(2.0 / math.pi) * (ys + 0.044715 * (ys * ys * ys))))
    gl = jnp.dot(hs.astype(BF16), wglu_ref[...], preferred_element_type=F32) + bglu_ref[...]
    hs = hs * jax.nn.sigmoid(gl)
    oa = jnp.dot(ha_ref[...], wa_ref[...], preferred_element_type=F32)
    ob = jnp.dot(hs.astype(BF16), wb_ref[...], preferred_element_type=F32)
    oc = jnp.dot(o_ref[...], wc_ref[...], preferred_element_type=F32)
    m = (jax.nn.sigmoid(ga_ref[...].astype(F32)) * oa + jax.nn.sigmoid(gb_ref[...].astype(F32)) * ob
         + jax.nn.sigmoid(gc_ref[...].astype(F32)) * oc)
    out = x_ref[...] + jnp.dot(m.astype(BF16), wo_ref[...], preferred_element_type=F32)
    out_ref[...] = out
    rows = out.shape[0]
    S = slab_ref.shape[0] // rows
    for j in range(S):
        slab_ref[pl.ds(j, rows, stride=S), :] = out[:, j * LANES:(j + 1) * LANES]


def _merge(x, ha, ys, z, ucol, gcol, o, d, wglu, bglu, wa, wb, wc, wo):
    T, D = x.shape
    C, S, AW = ha.shape[1], ys.shape[1], o.shape[1]
    assert ucol % S == 0 and gcol % D == 0
    tm = _tile(T, 256)
    ub, gb = ucol // S, gcol // D
    single = pl.Buffered(1)
    wspec = lambda a: pl.BlockSpec(a.shape, lambda i: (0, 0), pipeline_mode=single)
    return pl.pallas_call(
        _merge_kernel,
        grid=(T // tm,),
        in_specs=[
            pl.BlockSpec((tm, D), lambda i: (i, 0)),
            pl.BlockSpec((tm, C), lambda i: (i, 0)),
            pl.BlockSpec((tm, S), lambda i: (i, 0)),
            pl.BlockSpec((tm, S), lambda i: (i, ub)),
            pl.BlockSpec((tm, AW), lambda i: (i, 0)),
            pl.BlockSpec((tm, D), lambda i: (i, gb)),
            pl.BlockSpec((tm, D), lambda i: (i, gb + 1)),
            pl.BlockSpec((tm, D), lambda i: (i, gb + 2)),
            _const_spec((1, S)), wspec(wglu), _const_spec((1, S)),
            wspec(wa), wspec(wb), wspec(wc), wspec(wo),
        ],
        out_specs=[pl.BlockSpec((tm, D), lambda i: (i, 0)), pl.BlockSpec((tm * (D // LANES), LANES), lambda i: (i, 0))],
        out_shape=[jax.ShapeDtypeStruct((T, D), F32), jax.ShapeDtypeStruct((T * (D // LANES), LANES), F32)],
        compiler_params=_cparams("parallel"),
        name="merge",
    )(x, ha, ys, z, o, z, z, z, d[None], wglu, bglu[None], wa, wb, wc, wo)


def _route_kernel(x_ref, g_ref, wh_ref, wl_ref, b_ref, r_ref, *, ng, epg):
    x = x_ref[...]
    xn = x * lax.rsqrt(jnp.mean(x * x, axis=-1, keepdims=True) + EPS) * g_ref[...]
    xh = xn.astype(BF16)
    xl = (xn - xh.astype(F32)).astype(BF16)
    wh = wh_ref[...]
    lg = (jnp.dot(xh, wh, preferred_element_type=F32) + jnp.dot(xl, wh, preferred_element_type=F32)
          + jnp.dot(xh, wl_ref[...], preferred_element_type=F32) + b_ref[...])
    lane = lax.broadcasted_iota(jnp.int32, (1, LANES), 1).astype(F32)
    big = float(LANES)
    is_g = lane < ng
    gmax = jnp.max(jnp.where(is_g, lg, NEG), axis=-1, keepdims=True)
    den = jnp.sum(jnp.where(is_g, jnp.exp(jnp.where(is_g, lg, NEG) - gmax), 0.0), axis=-1, keepdims=True)
    p_top = 1.0 / den
    g_top = jnp.min(jnp.where(is_g & (lg == gmax), lane, big), axis=-1, keepdims=True)
    sel = (lane >= ng + g_top * epg) & (lane < ng + (g_top + 1.0) * epg)
    le = jnp.where(sel, lg, NEG)
    v1 = jnp.max(le, axis=-1, keepdims=True)
    i1 = jnp.min(jnp.where(sel & (le == v1), lane, big), axis=-1, keepdims=True)
    sel2 = sel & (lane != i1)
    le2 = jnp.where(sel2, lg, NEG)
    v2 = jnp.max(le2, axis=-1, keepdims=True)
    i2 = jnp.min(jnp.where(sel2 & (le2 == v2), lane, big), axis=-1, keepdims=True)
    t = jnp.exp(v2 - v1)
    w1 = p_top / (1.0 + t)
    w2 = p_top * t / (1.0 + t)
    r_ref[...] = jnp.where(lane == 0, i1 - ng, jnp.where(lane == 1, i2 - ng, jnp.where(lane == 2, w1, jnp.where(lane == 3, w2, 0.0))))


def _route(x, g, w_route_g, b_route_g, w_route_e, b_route_e):
    T, D = x.shape
    ng, ne = w_route_g.shape[1], w_route_e.shape[1]
    assert ng + ne <= LANES
    w = jnp.zeros((D, LANES), F32).at[:, :ng].set(w_route_g).at[:, ng:ng + ne].set(w_route_e)
    b = jnp.zeros((1, LANES), F32).at[0, :ng].set(b_route_g).at[0, ng:ng + ne].set(b_route_e)
    wh = w.astype(BF16)
    wl = (w - wh.astype(F32)).astype(BF16)
    tm = _tile(T, 512)
    return pl.pallas_call(
        functools.partial(_route_kernel, ng=ng, epg=ne // ng),
        grid=(T // tm,),
        in_specs=[pl.BlockSpec((tm, D), lambda i: (i, 0)), _const_spec((1, D)),
                  _const_spec((D, LANES)), _const_spec((D, LANES)), _const_spec((1, LANES))],
        out_specs=pl.BlockSpec((tm, LANES), lambda i: (i, 0)),
        out_shape=jax.ShapeDtypeStruct((T, LANES), F32),
        compiler_params=_cparams("parallel"),
        name="route",
    )(x, g[None], wh, wl, b)


def _dispatch_plan(route, n_exp, tmo):
    T = route.shape[0]
    eid = route[:, 0:2].astype(jnp.int32).reshape(-1)
    onehot = (eid[:, None] == jnp.arange(n_exp)[None, :]).astype(jnp.int32)
    csum = jnp.cumsum(onehot, axis=0)
    rank = jnp.sum((csum - onehot) * onehot, axis=1)
    counts = csum[-1]
    padded = ((counts + tmo - 1) // tmo) * tmo
    ends = jnp.cumsum(padded)
    starts = ends - padded
    pos = starts[eid] + rank
    NP = 2 * T + n_exp * tmo
    tok = jnp.zeros((NP,), jnp.int32).at[pos].set(jnp.arange(2 * T, dtype=jnp.int32) // 2)
    tile_start = jnp.arange(NP // tmo, dtype=jnp.int32) * tmo
    tile_exp = jnp.minimum(jnp.sum((tile_start[:, None] >= ends[None, :]).astype(jnp.int32), axis=1), n_exp - 1)
    n_used = (ends[-1] // tmo).astype(jnp.int32).reshape(1)
    return tok, tile_exp, n_used, pos.astype(jnp.int32)


def _pack_bf16_halves(y):
    half = y.shape[1] // 2
    bits = lambda a: lax.bitcast_convert_type(a.astype(BF16).astype(F32), jnp.uint32)
    return (bits(y[:, :half]) >> 16) | (bits(y[:, half:]) & jnp.uint32(0xFFFF0000))


def _unpack_bf16_halves(w):
    lo = lax.bitcast_convert_type(w << 16, F32)
    hi = lax.bitcast_convert_type(w & jnp.uint32(0xFFFF0000), F32)
    return lo, hi


def _moe_kernel(te_ref, tok_ref, nu_ref, x_hbm, g_ref, wg_ref, wu_ref, wd_ref, y_ref, xbuf, sem, *, tmo):
    i = pl.program_id(0)
    n_used = nu_ref[0]
    slot = i % 2

    S = xbuf.shape[1] // tmo

    def row_copy(tile, dst_slot, r):
        t = pl.multiple_of(tok_ref[tile * tmo + r] * S, S)
        return pltpu.make_async_copy(x_hbm.at[pl.ds(t, S), :], xbuf.at[dst_slot, pl.ds(r * S, S), :], sem.at[dst_slot])

    def experts(gather_next):
        if gather_next:
            for r in range(tmo):
                row_copy(i + 1, 1 - slot, r).start()
        pltpu.make_async_copy(x_hbm.at[pl.ds(0, tmo * S), :], xbuf.at[slot], sem.at[slot]).wait()
        x = jnp.concatenate([xbuf[slot, pl.ds(j, tmo, stride=S), :] for j in range(S)], axis=1)
        xn = (x * lax.rsqrt(jnp.mean(x * x, axis=-1, keepdims=True) + EPS) * g_ref[...]).astype(BF16)
        hg = jnp.dot(xn, wg_ref[0], preferred_element_type=F32)
        hu = jnp.dot(xn, wu_ref[0], preferred_element_type=F32)
        h = (hg * jax.nn.sigmoid(hg) * hu).astype(BF16)
        y = _pack_bf16_halves(jnp.dot(h, wd_ref[0], preferred_element_type=F32))
        sy = y_ref.shape[0] // tmo
        for j in range(sy):
            y_ref[pl.ds(j, tmo, stride=sy), :] = y[:, j * LANES:(j + 1) * LANES]

    @pl.when(i == 0)
    def _():
        def issue(r, c):
            row_copy(0, 0, r).start()
            return c
        lax.fori_loop(0, tmo, issue, 0)

    @pl.when(i + 1 < n_used)
    def _():
        experts(True)

    @pl.when(i + 1 == n_used)
    def _():
        experts(False)

    @pl.when(i >= n_used)
    def _():
        y_ref[...] = jnp.zeros_like(y_ref)


def _moe(x_slab, g, plan, wg, wu, wd, tmo):
    E, D, DE = wg.shape
    S, SY = D // LANES, D // 2 // LANES
    tok, tile_exp, n_used, _ = plan
    NP = tok.shape[0]
    grid_spec = pltpu.PrefetchScalarGridSpec(
        num_scalar_prefetch=3,
        grid=(NP // tmo,),
        in_specs=[
            pl.BlockSpec(memory_space=pl.ANY),
            pl.BlockSpec((1, D), lambda i, te, tk, nu: (0, 0)),
            pl.BlockSpec((1, D, DE), lambda i, te, tk, nu: (te[i], 0, 0)),
            pl.BlockSpec((1, D, DE), lambda i, te, tk, nu: (te[i], 0, 0)),
            pl.BlockSpec((1, DE, D), lambda i, te, tk, nu: (te[i], 0, 0)),
        ],
        out_specs=pl.BlockSpec((tmo * SY, LANES), lambda i, te, tk, nu: (i, 0)),
        scratch_shapes=[pltpu.VMEM((2, tmo * S, LANES), F32), pltpu.SemaphoreType.DMA((2,))],
    )
    return pl.pallas_call(
        functools.partial(_moe_kernel, tmo=tmo),
        grid_spec=grid_spec,
        out_shape=jax.ShapeDtypeStruct((NP * SY, LANES), jnp.uint32),
        compiler_params=_cparams("arbitrary"),
        name="moe_experts",
    )(tile_exp, tok, n_used, x_slab, g[None], wg, wu, wd)


def _combine_kernel(pos_ref, x_ref, r_ref, y_hbm, out_ref, ybuf, sem, *, tc):
    i = pl.program_id(0)
    n = pl.num_programs(0)
    slot = i % 2

    SY = ybuf.shape[2] // tc

    def row_copy(tile, dst_slot, r, s):
        p = pl.multiple_of(pos_ref[2 * (tile * tc + r) + s] * SY, SY)
        return pltpu.make_async_copy(y_hbm.at[pl.ds(p, SY), :], ybuf.at[dst_slot, s, pl.ds(r * SY, SY), :],
                                     sem.at[dst_slot])

    @pl.when(i == 0)
    def _():
        def issue(r, c):
            for s in range(2):
                row_copy(0, 0, r, s).start()
            return c
        lax.fori_loop(0, tc, issue, 0)

    @pl.when(i + 1 < n)
    def _():
        for r in range(tc):
            for s in range(2):
                row_copy(i + 1, 1 - slot, r, s).start()

    for s in range(2):
        pltpu.make_async_copy(y_hbm.at[pl.ds(0, tc * SY), :], ybuf.at[slot, s], sem.at[slot]).wait()
    half = out_ref.shape[1] // 2
    rows = lambda s: jnp.concatenate([ybuf[slot, s, pl.ds(j, tc, stride=SY), :] for j in range(SY)], axis=1)
    lo0, hi0 = _unpack_bf16_halves(rows(0))
    lo1, hi1 = _unpack_bf16_halves(rows(1))
    w0 = r_ref[:, 2:3]
    w1 = r_ref[:, 3:4]
    out_ref[:, :half] = x_ref[:, :half] + (w0 * lo0 + w1 * lo1)
    out_ref[:, half:] = x_ref[:, half:] + (w0 * hi0 + w1 * hi1)


def _combine(x, route, y_sorted, pos):
    T, D = x.shape
    tc = _tile(T, 256)
    grid_spec = pltpu.PrefetchScalarGridSpec(
        num_scalar_prefetch=1,
        grid=(T // tc,),
        in_specs=[pl.BlockSpec((tc, D), lambda i, p: (i, 0)), pl.BlockSpec((tc, LANES), lambda i, p: (i, 0)),
                  pl.BlockSpec(memory_space=pl.ANY)],
        out_specs=pl.BlockSpec((tc, D), lambda i, p: (i, 0)),
        scratch_shapes=[pltpu.VMEM((2, 2, tc * (D // 2 // LANES), LANES), jnp.uint32), pltpu.SemaphoreType.DMA((2,))],
    )
    return pl.pallas_call(
        functools.partial(_combine_kernel, tc=tc),
        grid_spec=grid_spec,
        out_shape=jax.ShapeDtypeStruct((T, D), F32),
        compiler_params=_cparams("arbitrary"),
        name="moe_combine",
    )(pos, x, route, y_sorted)


def _layer_weights(p, H):
    D = p['w_in'].shape[0]
    C = p['w_dw'].shape[1]
    S = p['ssm_d'].shape[0]
    AW = H * p['q_gain'].shape[0]
    o_u = 2 * C
    o_q = o_u + S
    o_f = o_q + 3 * AW
    o_g = o_f + H
    perm = lambda a: jnp.concatenate([a[..., o_g:], a[..., o_q:o_f], a[..., :o_u], a[..., o_u:o_q]], axis=-1)
    cols = dict(g=0, q=3 * D, conv=3 * D + 3 * AW, u=3 * D + 3 * AW + 2 * C)
    wf = jnp.zeros((D, LANES), F32).at[:, :H].set(p['w_in'][:, o_f:o_g]).astype(BF16)
    bf = jnp.zeros((1, LANES), F32).at[0, :H].set(p['b_in'][o_f:o_g])
    return perm(p['w_in']).astype(BF16), perm(p['b_in'])[None], wf, bf, cols


def _mixer_and_moe(x3, p, conv_buf, ssm_h0, past, H, layer, depth, kv_prev):
    B, L, D = x3.shape
    T = B * L
    x = x3.reshape(T, D)
    hd = p['q_gain'].shape[0]
    AW = H * hd
    K, C = p['w_dw'].shape
    G, P, gw = p['ssm_b_re'].shape
    S = G * gw
    w_main, b_main, wf, bf, cols = _layer_weights(p, H)
    z, logf = _in_proj(x, p['g_mix'][None], w_main, b_main, wf, bf)

    buf0 = jnp.zeros((B, CONV_HALO, C), F32)
    if conv_buf is not None:
        buf0 = buf0.at[:, CONV_HALO - (K - 1):].set(conv_buf)
    ha, nc = _conv_branch(z, cols['conv'], B, L, buf0, p['w_dw'], p['b_dw'], p['ln_g'], p['ln_b'])
    new_conv = nc[:, CONV_HALO - (K - 1):]

    mats = _ssm_matrices(p['ssm_a_re'], p['ssm_a_im'], p['ssm_log_dt'], p['ssm_b_re'], p['ssm_b_im'],
                         p['ssm_c_re'], p['ssm_c_im'])
    h0 = jnp.zeros((B, G, P, 2), F32) if ssm_h0 is None else ssm_h0
    if (L // SSM_CHUNK) % 8 == 0:
        ys, new_ssm = _ssm_branch_natural(z, cols['u'], B, L, mats, h0)
    else:
        ys, new_ssm = _ssm_branch(z[:, cols['u']:cols['u'] + S], B, L, mats, h0)

    vcol = cols['q'] + 2 * AW
    qa, ka, k5, v5 = _qk_prep(z, cols['q'], B, L, logf, p['q_gain'], p['k_gain'], H, layer, depth, kv_prev)
    if past is None:
        tq = _tile(L, 256, LANES)
        o = _attention(qa, ka, z, vcol, B, L, H, hd, tq, 2 * tq if L % (2 * tq) == 0 else tq)
    else:
        nk = ((L + LANES - 1) // LANES) * LANES
        pad_rows = lambda a: jnp.pad(a.reshape(B, L, -1), ((0, 0), (0, nk - L), (0, 0))).reshape(B * nk, -1)
        o = _cached_attention(qa, pad_rows(ka), pad_rows(z[:, vcol:vcol + AW]), *past, layer, B, L, H, hd)

    bfw = lambda a: a.astype(BF16)
    x, x_slab = _merge(x, ha, ys, z, cols['u'], cols['g'], o, p['ssm_d'], bfw(p['w_glu']), p['b_glu'],
               bfw(p['w_conv_out']), bfw(p['w_ssm_out']), bfw(p['w_attn_out']), bfw(p['w_out']))

    route = _route(x, p['g_ffn'], p['w_route_g'], p['b_route_g'], p['w_route_e'], p['b_route_e'])
    E = p['w_e_gate'].shape[0]
    tmo = _tile(T, 256 if 2 * T >= 256 * E else 64)
    plan = _dispatch_plan(route, E, tmo)
    y_sorted = _moe(x_slab, p['g_ffn'], plan, bfw(p['w_e_gate']), bfw(p['w_e_up']), bfw(p['w_e_down']), tmo)
    x = _combine(x, route, y_sorted, plan[3])

    state = (new_conv, new_ssm, logf[:, :H].reshape(B, L, H))
    return x.reshape(B, L, D), state, (k5, v5)


def kernel(x_prompt, x_sample, cache_conv, state_ssm, cache_k, cache_v, cache_logf, g_mix, w_in, b_in, w_dw, b_dw, ln_g, ln_b, w_conv_out, ssm_a_re, ssm_a_im, ssm_log_dt, ssm_b_re, ssm_b_im, ssm_c_re, ssm_c_im, ssm_d, w_glu, b_glu, w_ssm_out, q_gain, k_gain, w_attn_out, w_out, g_ffn, w_route_g, b_route_g, w_route_e, b_route_e, w_e_gate, w_e_up, w_e_down):
    depth = w_in.shape[0]
    H = cache_logf.shape[-1]
    stacked = dict(g_mix=g_mix, w_in=w_in, b_in=b_in, w_dw=w_dw, b_dw=b_dw, ln_g=ln_g, ln_b=ln_b,
                   w_conv_out=w_conv_out, ssm_a_re=ssm_a_re, ssm_a_im=ssm_a_im, ssm_log_dt=ssm_log_dt,
                   ssm_b_re=ssm_b_re, ssm_b_im=ssm_b_im, ssm_c_re=ssm_c_re, ssm_c_im=ssm_c_im, ssm_d=ssm_d,
                   w_glu=w_glu, b_glu=b_glu, w_ssm_out=w_ssm_out, q_gain=q_gain, k_gain=k_gain,
                   w_attn_out=w_attn_out, w_out=w_out, g_ffn=g_ffn, w_route_g=w_route_g, b_route_g=b_route_g,
                   w_route_e=w_route_e, b_route_e=b_route_e, w_e_gate=w_e_gate, w_e_up=w_e_up, w_e_down=w_e_down)
    xp, xs = x_prompt, x_sample
    st_p, st_s = [], []
    kv_p = kv_s = None
    nb, PL, hd = cache_k.shape[1], cache_k.shape[2], cache_k.shape[4]
    past = (jnp.transpose(cache_k, (0, 1, 3, 4, 2)).reshape(depth, nb, H * hd, PL),
            jnp.transpose(cache_v, (0, 1, 3, 4, 2)).reshape(depth, nb, H * hd, PL),
            jnp.transpose(cache_logf, (0, 1, 3, 2)))
    for l in range(depth):
        p = {k: v[l] for k, v in stacked.items()}
        xp, st, kv_p = _mixer_and_moe(xp, p, None, None, None, H, l, depth, kv_p)
        st_p.append(st)
        xs, st, kv_s = _mixer_and_moe(xs, p, cache_conv[l], state_ssm[l], past, H, l, depth, kv_s)
        st_s.append(st)
    stack = lambda sts, i: jnp.stack([s[i] for s in sts])

    def kv_out(a):
        return a if a.ndim == 5 else jnp.transpose(a.reshape(a.shape[0], a.shape[1], H, hd, a.shape[3]), (0, 1, 4, 2, 3))

    return ((xp, xs) + (stack(st_p, 0), stack(st_p, 1), kv_out(kv_p[0]), kv_out(kv_p[1]), stack(st_p, 2))
            + (stack(st_s, 0), stack(st_s, 1), kv_out(kv_s[0]), kv_out(kv_s[1]), stack(st_s, 2)))
```

```python
import functools
import math

import numpy as np
import jax
import jax.numpy as jnp
from jax import lax
from jax.experimental import pallas as pl
from jax.experimental.pallas import tpu as pltpu

F32 = jnp.float32
BF16 = jnp.bfloat16
EPS = 1e-6
NEG = -1e30
LANES = 128
CONV_HALO = 32
SSM_CHUNK = 16
MXU_DIM = 256
BIAS_SPLIT = 3
VMEM_LIMIT_BYTES = 56 * 1024 * 1024


def _cparams(*sem):
    return pltpu.CompilerParams(dimension_semantics=sem, vmem_limit_bytes=VMEM_LIMIT_BYTES)


def _tile(n, pref, mult=8):
    if n <= pref:
        return n
    t = (pref // mult) * mult
    while t >= mult:
        if n % t == 0:
            return t
        t -= mult
    return n


def _const_spec(shape):
    nd = len(shape)
    return pl.BlockSpec(shape, lambda *_: (0,) * nd)


def _in_proj_kernel(x_ref, g_ref, w_ref, b_ref, wf_ref, bf_ref, z_ref, lf_ref, xn_ref):
    @pl.when(pl.program_id(1) == 0)
    def _():
        x = x_ref[...]
        r = lax.rsqrt(jnp.mean(x * x, axis=-1, keepdims=True) + EPS)
        xn = (x * r * g_ref[...]).astype(BF16)
        xn_ref[...] = xn
        zf = jnp.dot(xn, wf_ref[...], preferred_element_type=F32) + bf_ref[...]
        lf_ref[...] = jnp.minimum(zf, 0.0) - jnp.log(1.0 + jnp.exp(-jnp.abs(zf)))

    z = jnp.dot(xn_ref[...], w_ref[...], preferred_element_type=F32) + b_ref[...]
    z_ref[...] = z.astype(z_ref.dtype)


def _in_proj(x, g, w, b, wf, bf):
    T, D = x.shape
    N = w.shape[1]
    tm = _tile(T, 1024)
    tn = _tile(N, 1536, LANES)
    return pl.pallas_call(
        _in_proj_kernel,
        grid=(T // tm, N // tn),
        in_specs=[
            pl.BlockSpec((tm, D), lambda i, j: (i, 0)),
            pl.BlockSpec((1, D), lambda i, j: (0, 0)),
            pl.BlockSpec((D, tn), lambda i, j: (0, j)),
            pl.BlockSpec((1, tn), lambda i, j: (0, j)),
            pl.BlockSpec((D, LANES), lambda i, j: (0, 0)),
            pl.BlockSpec((1, LANES), lambda i, j: (0, 0)),
        ],
        out_specs=[
            pl.BlockSpec((tm, tn), lambda i, j: (i, j)),
            pl.BlockSpec((tm, LANES), lambda i, j: (i, 0)),
        ],
        out_shape=[jax.ShapeDtypeStruct((T, N), BF16), jax.ShapeDtypeStruct((T, LANES), F32)],
        scratch_shapes=[pltpu.VMEM((tm, D), BF16)],
        compiler_params=_cparams("parallel", "arbitrary"),
        name="in_proj",
    )(x, g, w, b, wf, bf)


def _conv_kernel(a_ref, b_ref, buf0_ref, w_ref, bdw_ref, lng_ref, lnb_ref, h_ref, nc_ref, u_ref, acc_ref,
                 *, tt, C, K):
    ti = pl.program_id(1)

    @pl.when(ti == 0)
    def _():
        u_ref[0:CONV_HALO, :] = buf0_ref[0]

    @pl.when(ti > 0)
    def _():
        u_ref[0:CONV_HALO, :] = u_ref[tt:tt + CONV_HALO, :]

    u_ref[CONV_HALO:CONV_HALO + tt, :] = a_ref[...].astype(F32) * jax.nn.sigmoid(b_ref[...].astype(F32))
    base = CONV_HALO - (K - 1)
    rc = min(tt, 128)
    for r0 in range(0, tt, rc):
        for c0 in range(0, C, LANES):
            acc = jnp.zeros((rc, LANES), F32)
            for j in range(K):
                acc = acc + w_ref[j:j + 1, c0:c0 + LANES] * u_ref[r0 + base + j:r0 + base + j + rc, c0:c0 + LANES]
            acc_ref[r0:r0 + rc, c0:c0 + LANES] = acc
    h = acc_ref[...] + bdw_ref[...]
    mu = jnp.mean(h, axis=-1, keepdims=True)
    d = h - mu
    var = jnp.mean(d * d, axis=-1, keepdims=True)
    y = d * lax.rsqrt(var + EPS) * lng_ref[...] + lnb_ref[...]
    h_ref[...] = (y * jax.nn.sigmoid(y)).astype(h_ref.dtype)

    @pl.when(ti == pl.num_programs(1) - 1)
    def _():
        nc_ref[0] = u_ref[tt:tt + CONV_HALO, :]


def _conv_branch(z, col0, B, L, buf0, w_dw, b_dw, ln_g, ln_b):
    K, C = w_dw.shape
    assert K - 1 <= CONV_HALO and L >= CONV_HALO and C % LANES == 0 and col0 % C == 0
    tt = _tile(L, 256)
    nt = L // tt
    cb = col0 // C
    wpad = jnp.zeros((CONV_HALO, C), F32).at[:K].set(w_dw)
    kern = functools.partial(_conv_kernel, tt=tt, C=C, K=K)
    return pl.pallas_call(
        kern,
        grid=(B, nt),
        in_specs=[
            pl.BlockSpec((tt, C), lambda b, t: (b * nt + t, cb)),
            pl.BlockSpec((tt, C), lambda b, t: (b * nt + t, cb + 1)),
            pl.BlockSpec((1, CONV_HALO, C), lambda b, t: (b, 0, 0)),
            _const_spec((CONV_HALO, C)),
            _const_spec((1, C)), _const_spec((1, C)), _const_spec((1, C)),
        ],
        out_specs=[
            pl.BlockSpec((tt, C), lambda b, t: (b * nt + t, 0)),
            pl.BlockSpec((1, CONV_HALO, C), lambda b, t: (b, 0, 0)),
        ],
        out_shape=[jax.ShapeDtypeStruct((B * L, C), BF16), jax.ShapeDtypeStruct((B, CONV_HALO, C), F32)],
        scratch_shapes=[pltpu.VMEM((CONV_HALO + tt, C), F32), pltpu.VMEM((tt, C), F32)],
        compiler_params=_cparams("parallel", "arbitrary"),
        name="conv_branch",
    )(z, z, buf0, wpad, b_dw[None], ln_g[None], ln_b[None])


def _ssm_matrices(a_re, a_im, log_dt, b_re, b_im, c_re, c_im):
    hp = lax.Precision.HIGHEST
    G, P, gw = b_re.shape
    S = SSM_CHUNK
    lam_re = jnp.minimum(a_re, -1e-4)
    lam_im = a_im
    dt = jnp.exp(log_dt)[:, None]
    d = jnp.arange(S + 1, dtype=F32)[None, :, None]
    mag = jnp.exp(lam_re[:, None, :] * dt[:, None, :] * d)
    ang = lam_im[:, None, :] * dt[:, None, :] * d
    pw_re, pw_im = mag * jnp.cos(ang), mag * jnp.sin(ang)
    den = lam_re * lam_re + lam_im * lam_im
    xr, xi = pw_re[:, 1] - 1.0, pw_im[:, 1]
    cf_re = (xr * lam_re + xi * lam_im) / den
    cf_im = (xi * lam_re - xr * lam_im) / den
    bb_re = cf_re[..., None] * b_re - cf_im[..., None] * b_im
    bb_im = cf_re[..., None] * b_im + cf_im[..., None] * b_re
    cp_re = c_re[:, None] * pw_re[:, :, None, :] - c_im[:, None] * pw_im[:, :, None, :]
    cp_im = c_re[:, None] * pw_im[:, :, None, :] + c_im[:, None] * pw_re[:, :, None, :]
    kk = (jnp.einsum('gdcp,gpe->gdce', cp_re[:, :S], bb_re, precision=hp)
          - jnp.einsum('gdcp,gpe->gdce', cp_im[:, :S], bb_im, precision=hp))
    s_idx = jnp.arange(S)[:, None]
    t_idx = jnp.arange(S)[None, :]
    lag = jnp.clip(t_idx - s_idx, 0, S - 1)
    m = kk[:, lag] * (t_idx >= s_idx)[None, :, :, None, None]
    m = m.transpose(0, 1, 4, 2, 3).reshape(G, S * gw, S * gw)
    rev_re = pw_re[:, S - 1 - jnp.arange(S)]
    rev_im = pw_im[:, S - 1 - jnp.arange(S)]
    p_re = rev_re[:, :, None, :] * bb_re.transpose(0, 2, 1)[:, None] - rev_im[:, :, None, :] * bb_im.transpose(0, 2, 1)[:, None]
    p_im = rev_re[:, :, None, :] * bb_im.transpose(0, 2, 1)[:, None] + rev_im[:, :, None, :] * bb_re.transpose(0, 2, 1)[:, None]
    p_re = p_re.reshape(G, S * gw, P)
    p_im = p_im.reshape(G, S * gw, P)
    w1 = jnp.concatenate([m, p_re, p_im, p_im, p_re], axis=-1)
    q_re = cp_re[:, 1:].transpose(0, 3, 1, 2).reshape(G, P, S * gw)
    q_im = -cp_im[:, 1:].transpose(0, 3, 1, 2).reshape(G, P, S * gw)
    q = jnp.concatenate([q_re, q_im], axis=1)
    ar, ai = pw_re[:, S], pw_im[:, S]
    a = jnp.zeros((G, 8, 2 * P), F32)
    a = a.at[:, 0].set(jnp.concatenate([ar, ar], -1))
    a = a.at[:, 1].set(jnp.concatenate([-ai, ai], -1))
    a = a.at[:, 2].set(jnp.concatenate([ai, -ai], -1))
    return w1.astype(BF16), q.astype(BF16), a


def _ssm_kernel(u_ref, w1_ref, q_ref, a_ref, h0_ref, h0s_ref, y_ref, hf_ref, tmp_ref, hh_ref, *, nk, B, CW, P2):
    tmp_ref[...] = jnp.dot(u_ref[0], w1_ref[0], preferred_element_type=F32)
    a1 = a_ref[0, 0:1, :]
    a2 = a_ref[0, 1:2, :]
    a2s = a_ref[0, 2:3, :]

    def body(k, carry):
        h, hs = carry
        r = pl.multiple_of(k * B, B)
        hh_ref[pl.ds(r, B), :] = h
        inc = tmp_ref[pl.ds(r, B), CW:CW + P2]
        incs = tmp_ref[pl.ds(r, B), CW + P2:CW + 2 * P2]
        return a1 * h + a2 * hs + inc, a1 * hs + a2s * h + incs

    h, _ = lax.fori_loop(0, nk, body, (h0_ref[0], h0s_ref[0]))
    hf_ref[0] = h
    y = tmp_ref[:, 0:CW] + jnp.dot(hh_ref[...].astype(BF16), q_ref[0], preferred_element_type=F32)
    y_ref[0] = y.astype(y_ref.dtype)


def _ssm_branch(u, B, L, mats, h0):
    w1, q, a = mats
    G, CW, _ = w1.shape
    P2 = q.shape[1]
    gw = CW // SSM_CHUNK
    assert CW == MXU_DIM and L % SSM_CHUNK == 0
    nk = L // SSM_CHUNK
    R = nk * B
    ug = u.reshape(B, nk, SSM_CHUNK, G, gw).transpose(3, 1, 0, 2, 4).reshape(G, R, CW)
    h0r = jnp.concatenate([h0[..., 0], h0[..., 1]], -1).transpose(1, 0, 2)
    h0s = jnp.concatenate([h0[..., 1], h0[..., 0]], -1).transpose(1, 0, 2)
    kern = functools.partial(_ssm_kernel, nk=nk, B=B, CW=CW, P2=P2)
    y, hf = pl.pallas_call(
        kern,
        grid=(G,),
        in_specs=[
            pl.BlockSpec((1, R, CW), lambda g: (g, 0, 0)),
            pl.BlockSpec((1, CW, CW + 2 * P2), lambda g: (g, 0, 0)),
            pl.BlockSpec((1, P2, CW), lambda g: (g, 0, 0)),
            pl.BlockSpec((1, 8, P2), lambda g: (g, 0, 0)),
            pl.BlockSpec((1, B, P2), lambda g: (g, 0, 0)),
            pl.BlockSpec((1, B, P2), lambda g: (g, 0, 0)),
        ],
        out_specs=[
            pl.BlockSpec((1, R, CW), lambda g: (g, 0, 0)),
            pl.BlockSpec((1, B, P2), lambda g: (g, 0, 0)),
        ],
        out_shape=[jax.ShapeDtypeStruct((G, R, CW), BF16), jax.ShapeDtypeStruct((G, B, P2), F32)],
        scratch_shapes=[pltpu.VMEM((R, CW + 2 * P2), F32), pltpu.VMEM((R, P2), F32)],
        compiler_params=_cparams("parallel"),
        name="ssm_branch",
    )(ug, w1, q, a, h0r, h0s)
    y = y.reshape(G, nk, B, SSM_CHUNK, gw).transpose(2, 1, 3, 0, 4).reshape(B * L, G * gw)
    P = P2 // 2
    new_state = jnp.stack([hf[..., :P], hf[..., P:]], axis=-1).transpose(1, 0, 2, 3)
    return y, new_state


def _ssm_seq_kernel(u_ref, w1_ref, q_ref, a_ref, h0_ref, h0s_ref, y_ref, hf_ref, uf_ref, tmp_ref, hh_ref, yg_ref,
                    *, nk, G, gw, CW, P2):
    S = SSM_CHUNK
    per = LANES // gw
    lane = lax.broadcasted_iota(jnp.int32, (1, LANES), 1)
    slot = [(lane >= i * gw) & (lane < (i + 1) * gw) for i in range(per)]
    for j in range(G // per):
        uf_ref[j] = u_ref[:, j * LANES:(j + 1) * LANES].astype(F32)

    for j in range(G // per):
        xs = [uf_ref[j, pl.ds(s, nk, stride=S), :] for s in range(S)]
        for g8 in range(per):
            g = j * per + g8
            halves = []
            for d in range(S // per):
                acc = jnp.zeros((nk, LANES), F32)
                for s8 in range(per):
                    x = xs[d * per + s8]
                    sh = ((s8 - g8) * gw) % LANES
                    acc = jnp.where(slot[s8], pltpu.roll(x, sh, axis=1) if sh else x, acc)
                halves.append(acc)
            ug = jnp.concatenate(halves, axis=1).astype(BF16)
            tmp_ref[:, g, :] = jnp.dot(ug, w1_ref[g], preferred_element_type=F32)

    a1, a2, a2s = a_ref[0], a_ref[1], a_ref[2]

    def body(k, carry):
        h, hs = carry
        hh_ref[k] = h
        inc = tmp_ref[k, :, CW:CW + P2]
        incs = tmp_ref[k, :, CW + P2:CW + 2 * P2]
        return a1 * h + a2 * hs + inc, a1 * hs + a2s * h + incs

    h, _ = lax.fori_loop(0, nk, body, (h0_ref[0], h0s_ref[0]))
    hf_ref[0] = h

    for g in range(G):
        yg_ref[g] = tmp_ref[:, g, 0:CW] + jnp.dot(hh_ref[:, g, :].astype(BF16), q_ref[g], preferred_element_type=F32)

    for j in range(G // per):
        for t in range(S):
            d, t8 = divmod(t, per)
            acc = jnp.zeros((nk, LANES), F32)
            for g8 in range(per):
                x = yg_ref[j * per + g8, :, d * LANES:(d + 1) * LANES]
                sh = ((g8 - t8) * gw) % LANES
                acc = jnp.where(slot[g8], pltpu.roll(x, sh, axis=1) if sh else x, acc)
            uf_ref[j, pl.ds(t, nk, stride=S), :] = acc
    for j in range(G // per):
        y_ref[:, j * LANES:(j + 1) * LANES] = uf_ref[j].astype(y_ref.dtype)


def _ssm_branch_natural(z, ucol, B, L, mats, h0):
    w1, q, a = mats
    G, CW, _ = w1.shape
    P2 = q.shape[1]
    gw = CW // SSM_CHUNK
    SW = G * gw
    nk = L // SSM_CHUNK
    assert CW == MXU_DIM and L % SSM_CHUNK == 0 and nk % 8 == 0 and LANES % gw == 0 and SW % LANES == 0
    assert SSM_CHUNK % (LANES // gw) == 0 and ucol % SW == 0
    h0r = jnp.concatenate([h0[..., 0], h0[..., 1]], -1)
    h0s = jnp.concatenate([h0[..., 1], h0[..., 0]], -1)
    a3 = a[:, :3].transpose(1, 0, 2)
    kern = functools.partial(_ssm_seq_kernel, nk=nk, G=G, gw=gw, CW=CW, P2=P2)
    single = pl.Buffered(1)
    y, hf = pl.pallas_call(
        kern,
        grid=(B,),
        in_specs=[
            pl.BlockSpec((L, SW), lambda b: (b, ucol // SW)),
            pl.BlockSpec(w1.shape, lambda b: (0, 0, 0), pipeline_mode=single),
            pl.BlockSpec(q.shape, lambda b: (0, 0, 0), pipeline_mode=single),
            pl.BlockSpec(a3.shape, lambda b: (0, 0, 0)),
            pl.BlockSpec((1, G, P2), lambda b: (b, 0, 0)),
            pl.BlockSpec((1, G, P2), lambda b: (b, 0, 0)),
        ],
        out_specs=[
            pl.BlockSpec((L, SW), lambda b: (b, 0)),
            pl.BlockSpec((1, G, P2), lambda b: (b, 0, 0)),
        ],
        out_shape=[jax.ShapeDtypeStruct((B * L, SW), BF16), jax.ShapeDtypeStruct((B, G, P2), F32)],
        scratch_shapes=[pltpu.VMEM((SW // LANES, L, LANES), F32), pltpu.VMEM((nk, G, CW + 2 * P2), F32),
                        pltpu.VMEM((nk, G, P2), F32), pltpu.VMEM((G, nk, CW), F32)],
        compiler_params=_cparams("parallel"),
        name="ssm_branch_seq",
    )(z, w1, q, a3, h0r, h0s)
    P = P2 // 2
    return y, jnp.stack([hf[..., :P], hf[..., P:]], axis=-1)


def _bias_table(H):
    assert H % 2 == 0 and BIAS_SPLIT * H <= LANES
    pk = np.zeros((LANES, (H // 2) * LANES), np.float32)
    for h in range(H):
        p, e = divmod(h, 2)
        for i in range(BIAS_SPLIT):
            pk[i * H + h, p * LANES + e * BIAS_SPLIT + i] = -1.0
    return jnp.asarray(pk, BF16)


def _stack_pair_queries(q, e, hd):
    lane = lax.broadcasted_iota(jnp.int32, (1, LANES), 1)
    qe = jnp.where((lane >= e * hd) & (lane < (e + 1) * hd), q, jnp.zeros_like(q))
    one = jnp.where((lane >= e * BIAS_SPLIT) & (lane < (e + 1) * BIAS_SPLIT), 1.0, 0.0).astype(q.dtype)
    return jnp.concatenate([qe, jnp.broadcast_to(one, q.shape)], axis=1)


def _split3(x):
    hi = x.astype(BF16)
    r1 = x - hi.astype(F32)
    mid = r1.astype(BF16)
    lo = (r1 - mid.astype(F32)).astype(BF16)
    return hi, mid, lo


def _forget_cumsum(lf, carry, H):
    tl = lf.shape[0]
    lane = lax.broadcasted_iota(jnp.int32, (1, LANES), 1)
    lf = jnp.where(lane < H, lf, 0.0)
    tri = (lax.broadcasted_iota(jnp.int32, (tl, tl), 0) >= lax.broadcasted_iota(jnp.int32, (tl, tl), 1)).astype(BF16)
    hi, mid, lo = _split3(lf)
    c = (jnp.dot(tri, hi, preferred_element_type=F32) + jnp.dot(tri, mid, preferred_element_type=F32)
         + jnp.dot(tri, lo, preferred_element_type=F32))
    return c + carry


def _pack_split(c, H):
    hi, mid, lo = _split3(c)
    packed = hi.astype(F32) + pltpu.roll(mid.astype(F32), H, axis=1) + pltpu.roll(lo.astype(F32), 2 * H, axis=1)
    return packed.astype(BF16)


def _pair_rmsnorm(x, gain, lo_half, hd):
    sq = x * x
    s_lo = jnp.sum(jnp.where(lo_half, sq, 0.0), axis=-1, keepdims=True)
    s_hi = jnp.sum(jnp.where(lo_half, 0.0, sq), axis=-1, keepdims=True)
    ms = jnp.where(lo_half, s_lo, s_hi) * (1.0 / hd)
    return x * lax.rsqrt(ms + EPS) * gain


def _qk_kernel(*refs, tl, H, hd, scale, aliased, time_minor, own):
    zq_ref, zk_ref, zv_ref, lf_ref, qg_ref, kg_ref, pk_ref = refs[:7]
    qa_ref, ka_ref, k5_ref, v5_ref, c_ref = refs[9:] if aliased else refs[7:]

    @pl.when(pl.program_id(1) == 0)
    def _():
        c_ref[...] = jnp.zeros_like(c_ref)

    for other in range(k5_ref.shape[0]):
        if other != own:
            k5_ref[other] = jnp.zeros(k5_ref.shape[1:], F32)
            v5_ref[other] = jnp.zeros(v5_ref.shape[1:], F32)

    c = _forget_cumsum(lf_ref[...], c_ref[0:1, :], H)
    c_ref[...] = jnp.broadcast_to(c[tl - 1:tl, :], c_ref.shape)
    bk = jnp.dot(_pack_split(c, H), pk_ref[...], preferred_element_type=F32)
    lo_half = lax.broadcasted_iota(jnp.int32, (1, LANES), 1) < hd
    for p in range(H // 2):
        sl = slice(p * LANES, (p + 1) * LANES)
        qn = _pair_rmsnorm(zq_ref[:, sl].astype(F32), qg_ref[:, sl], lo_half, hd) * scale
        kn = _pair_rmsnorm(zk_ref[:, sl].astype(F32), kg_ref[:, sl], lo_half, hd)
        vv = zv_ref[:, sl].astype(F32)
        qa_ref[:, sl] = qn.astype(BF16)
        ka_ref[:, 2 * p * LANES:(2 * p + 1) * LANES] = kn.astype(BF16)
        ka_ref[:, (2 * p + 1) * LANES:(2 * p + 2) * LANES] = bk[:, sl].astype(BF16)
        if time_minor:
            k5_ref[own, 0, sl, :] = kn.T
            v5_ref[own, 0, sl, :] = vv.T
        else:
            for e in range(2):
                k5_ref[own, 0, :, 2 * p + e, :] = kn[:, e * hd:(e + 1) * hd]
                v5_ref[own, 0, :, 2 * p + e, :] = vv[:, e * hd:(e + 1) * hd]


def _qk_prep(z, qcol, B, L, logf, q_gain, k_gain, H, layer, depth, kv_prev):
    hd = q_gain.shape[0]
    AW = H * hd
    assert 2 * hd == LANES and qcol % AW == 0
    tl = _tile(L, 256)
    nt = L // tl
    qb = qcol // AW
    pk = _bias_table(H)
    NA = H * LANES
    aliased = kv_prev is not None
    time_minor = tl % LANES == 0
    nl, l0 = (1, layer) if aliased else (depth, 0)
    kern = functools.partial(_qk_kernel, tl=tl, H=H, hd=hd, scale=hd ** -0.5, aliased=aliased,
                             time_minor=time_minor, own=layer - l0)
    row = lambda b, t: b * nt + t
    in_specs = [
        pl.BlockSpec((tl, AW), lambda b, t: (row(b, t), qb)),
        pl.BlockSpec((tl, AW), lambda b, t: (row(b, t), qb + 1)),
        pl.BlockSpec((tl, AW), lambda b, t: (row(b, t), qb + 2)),
        pl.BlockSpec((tl, LANES), lambda b, t: (row(b, t), 0)),
        _const_spec((1, AW)), _const_spec((1, AW)), _const_spec(pk.shape),
    ]
    args = [z, z, z, logf, jnp.tile(q_gain, H)[None], jnp.tile(k_gain, H)[None], pk]
    if aliased:
        in_specs += [pl.BlockSpec(memory_space=pl.ANY), pl.BlockSpec(memory_space=pl.ANY)]
        args += list(kv_prev)
    if time_minor:
        kv_spec = pl.BlockSpec((nl, 1, AW, tl), lambda b, t: (l0, b, 0, t))
        kv_shape = jax.ShapeDtypeStruct((depth, B, AW, L), F32)
    else:
        kv_spec = pl.BlockSpec((nl, 1, tl, H, hd), lambda b, t: (l0, b, t, 0, 0))
        kv_shape = jax.ShapeDtypeStruct((depth, B, L, H, hd), F32)
    return pl.pallas_call(
        kern,
        grid=(B, nt),
        in_specs=in_specs,
        out_specs=[
            pl.BlockSpec((tl, AW), lambda b, t: (row(b, t), 0)),
            pl.BlockSpec((tl, NA), lambda b, t: (row(b, t), 0)),
            kv_spec, kv_spec,
        ],
        out_shape=[jax.ShapeDtypeStruct((B * L, AW), BF16), jax.ShapeDtypeStruct((B * L, NA), BF16),
                   kv_shape, kv_shape],
        scratch_shapes=[pltpu.VMEM((8, LANES), F32)],
        input_output_aliases={7: 2, 8: 3} if aliased else {},
        compiler_params=_cparams("parallel", "arbitrary"),
        name="qk_prep",
    )(*args)


def _attn_kernel(q_ref, k_ref, v_ref, o_ref, q2_ref, acc_ref, m_ref, *, tq, tk, hd):
    qi = pl.program_id(2)
    npp = q2_ref.shape[0]
    for pp in range(npp):
        q = q_ref[:, pp * LANES:(pp + 1) * LANES]
        for e in range(2):
            q2_ref[pp, e * tq:(e + 1) * tq, :] = _stack_pair_queries(q, e, hd)
    q_lo = pl.multiple_of(qi * tq, tq)
    m_ref[...] = jnp.full(m_ref.shape, NEG, F32)
    acc_ref[...] = jnp.zeros(acc_ref.shape, F32)

    def step(k0, width, masked):
        mask = None
        if masked:
            row = lax.broadcasted_iota(jnp.int32, (2 * tq, 1), 0)
            qpos = q_lo + jnp.where(row >= tq, row - tq, row)
            mask = k0 + lax.broadcasted_iota(jnp.int32, (1, width), 1) <= qpos
        ones = jnp.ones((width, LANES), BF16)
        chains = [(q2_ref[pp], k_ref[pl.ds(k0, width), 2 * pp * LANES:(2 * pp + 2) * LANES],
                   jnp.concatenate([v_ref[pl.ds(k0, width), pp * LANES:(pp + 1) * LANES], ones], axis=1),
                   m_ref.at[pp], acc_ref.at[pp]) for pp in range(npp)]
        _softmax_step(chains, mask)

    def full_body(j, c):
        step(pl.multiple_of(j * tk, tk), tk, False)
        return c

    n_wide = q_lo // tk
    lax.fori_loop(0, n_wide, full_body, 0)
    if tk != tq:
        @pl.when(q_lo - n_wide * tk >= tq)
        def _():
            step(pl.multiple_of(n_wide * tk, tq), tq, False)
    step(q_lo, tq, True)
    for pp in range(npp):
        o_ref[:, pp * LANES:(pp + 1) * LANES] = _pair_output(acc_ref[pp], tq, hd).astype(o_ref.dtype)


def _softmax_step(chains, mask):
    nt_dims = (((1,), (1,)), ((), ()))
    scores = [lax.dot_general(q2, k_aug, nt_dims, preferred_element_type=F32) for q2, k_aug, _, _, _ in chains]
    probs = []
    for s, (_, k_aug, _, m_ref, _) in zip(scores, chains):
        if mask is not None:
            s = jnp.where(mask, s, NEG)
        m_prev = m_ref[...]
        m_next = jnp.maximum(m_prev, jnp.max(s, axis=-1, keepdims=True))
        alpha = jnp.exp(m_prev - m_next)
        p = jnp.exp((s - jnp.concatenate([m_next] * (k_aug.shape[0] // LANES), axis=1)).astype(BF16))
        m_ref[...] = m_next
        probs.append((p, alpha))
    for (p, alpha), (_, _, v_aug, _, acc_ref) in zip(probs, chains):
        pv = jnp.dot(p, v_aug, preferred_element_type=F32)
        acc_ref[...] = jnp.concatenate([alpha, alpha], axis=1) * acc_ref[...] + pv


def _pair_output(acc, tq, hd):
    o = acc[:, 0:LANES] / acc[:, LANES:2 * LANES]
    lane = lax.broadcasted_iota(jnp.int32, (1, LANES), 1)
    return jnp.where(lane < hd, o[0:tq], o[tq:2 * tq])


def _attention(qa, ka, v, vcol, B, L, H, hd, tq, tk):
    npair = H // 2
    npp = 4 if npair % 4 == 0 else (2 if npair % 2 == 0 else 1)
    nq = L // tq
    assert L % tq == 0 and tk in (tq, 2 * tq) and tq % LANES == 0 and vcol % (npp * LANES) == 0
    vb = vcol // (npp * LANES)
    kern = functools.partial(_attn_kernel, tq=tq, tk=tk, hd=hd)
    return pl.pallas_call(
        kern,
        grid=(B, npair // npp, nq),
        in_specs=[
            pl.BlockSpec((tq, npp * LANES), lambda b, p, i: (b * nq + i, p)),
            pl.BlockSpec((L, npp * 2 * LANES), lambda b, p, i: (b, p)),
            pl.BlockSpec((L, npp * LANES), lambda b, p, i: (b, vb + p)),
        ],
        out_specs=pl.BlockSpec((tq, npp * LANES), lambda b, p, i: (b * nq + i, p)),
        out_shape=jax.ShapeDtypeStruct((B * L, H * hd), BF16),
        scratch_shapes=[pltpu.VMEM((npp, 2 * tq, 2 * LANES), BF16), pltpu.VMEM((npp, 2 * tq, 2 * LANES), F32),
                        pltpu.VMEM((npp, 2 * tq, LANES), F32)],
        compiler_params=_cparams("parallel", "parallel", "arbitrary"),
        name="fox_attention",
    )(qa, ka, v)


def _cached_attn_kernel(q_ref, kc_ref, vc_ref, lf_ref, kn_ref, vn_ref, o_ref,
                        q2_ref, acc_ref, m_ref, l_ref, c_ref, *, Lq, tl, H, hd):
    t = pl.program_id(1)
    npair = H // 2
    row = lax.broadcasted_iota(jnp.int32, (2 * Lq, 1), 0)

    @pl.when(t == 0)
    def _():
        c_ref[...] = jnp.zeros_like(c_ref)
        m_ref[...] = jnp.full(m_ref.shape, NEG, F32)
        l_ref[...] = jnp.zeros(l_ref.shape, F32)
        acc_ref[...] = jnp.zeros(acc_ref.shape, F32)
        for p in range(npair):
            q = q_ref[:, p * LANES:(p + 1) * LANES]
            for e in range(2):
                q2_ref[p, e * Lq:(e + 1) * Lq, :] = _stack_pair_queries(q, e, hd)

    def update(p, s, weigh):
        m_prev = m_ref[p]
        m_next = jnp.maximum(m_prev, jnp.max(s, axis=-1, keepdims=True))
        alpha = jnp.exp(m_prev - m_next)
        pr = jnp.exp(s - jnp.concatenate([m_next] * (s.shape[1] // LANES), axis=1))
        l_ref[p] = alpha * l_ref[p] + jnp.sum(pr, axis=-1, keepdims=True)
        acc_ref[p] = alpha * acc_ref[p] + weigh(pr.astype(BF16))
        m_ref[p] = m_next

    lf = lf_ref[0, 0]
    after = (lax.broadcasted_iota(jnp.int32, (tl, tl), 0) > lax.broadcasted_iota(jnp.int32, (tl, tl), 1)).astype(BF16)
    later = sum(jnp.dot(piece, after, preferred_element_type=F32) for piece in _split3(lf))
    later = later + jnp.concatenate([c_ref[...]] * (tl // LANES), axis=1)
    c_ref[...] = c_ref[...] + jnp.sum(lf, axis=-1, keepdims=True)
    nt_dims = (((1,), (1,)), ((), ()))
    for p in range(npair):
        sl = slice(p * LANES, (p + 1) * LANES)
        kp = kc_ref[0, 0, sl, :].astype(BF16)
        vp = vc_ref[0, 0, sl, :].astype(BF16)
        s = jnp.dot(q2_ref[p, :, 0:LANES], kp, preferred_element_type=F32)
        s = s + jnp.where(row < Lq, later[2 * p:2 * p + 1, :], later[2 * p + 1:2 * p + 2, :])
        update(p, s, lambda pr: lax.dot_general(pr, vp, nt_dims, preferred_element_type=F32))

    @pl.when(t == pl.num_programs(1) - 1)
    def _():
        nk = kn_ref.shape[0]
        mask = lax.broadcasted_iota(jnp.int32, (1, nk), 1) <= jnp.where(row >= Lq, row - Lq, row)
        lane = lax.broadcasted_iota(jnp.int32, (1, LANES), 1)
        for p in range(npair):
            s = lax.dot_general(q2_ref[p], kn_ref[:, 2 * p * LANES:(2 * p + 2) * LANES], nt_dims,
                                preferred_element_type=F32)
            vn = vn_ref[:, p * LANES:(p + 1) * LANES]
            update(p, jnp.where(mask, s, NEG), lambda pr: jnp.dot(pr, vn, preferred_element_type=F32))
            o = acc_ref[p] / l_ref[p]
            o_ref[:, p * LANES:(p + 1) * LANES] = jnp.where(lane < hd, o[0:Lq], o[Lq:2 * Lq]).astype(o_ref.dtype)


def _cached_attention(qa, ka_new, v_new, cache_kt, cache_vt, cache_lft, layer, B, Lq, H, hd):
    PL = cache_kt.shape[-1]
    AW = H * hd
    NA = H * LANES
    nk = ka_new.shape[0] // B
    tl = _tile(PL, 1024, LANES)
    nt = PL // tl
    npair = H // 2
    kern = functools.partial(_cached_attn_kernel, Lq=Lq, tl=tl, H=H, hd=hd)
    kv_spec = pl.BlockSpec((1, 1, AW, tl), lambda b, t: (layer, b, 0, nt - 1 - t))
    return pl.pallas_call(
        kern,
        grid=(B, nt),
        in_specs=[
            pl.BlockSpec((Lq, AW), lambda b, t: (b, 0)),
            kv_spec, kv_spec,
            pl.BlockSpec((1, 1, H, tl), lambda b, t: (layer, b, 0, nt - 1 - t)),
            pl.BlockSpec((nk, NA), lambda b, t: (b, 0)),
            pl.BlockSpec((nk, AW), lambda b, t: (b, 0)),
        ],
        out_specs=pl.BlockSpec((Lq, AW), lambda b, t: (b, 0)),
        out_shape=jax.ShapeDtypeStruct((B * Lq, AW), BF16),
        scratch_shapes=[pltpu.VMEM((npair, 2 * Lq, 2 * LANES), BF16), pltpu.VMEM((npair, 2 * Lq, LANES), F32),
                        pltpu.VMEM((npair, 2 * Lq, LANES), F32), pltpu.VMEM((npair, 2 * Lq, LANES), F32),
                        pltpu.VMEM((H, LANES), F32)],
        compiler_params=_cparams("parallel", "arbitrary"),
        name="fox_cached_attention",
    )(qa, cache_kt, cache_vt, cache_lft, ka_new, v_new)


def _merge_kernel(x_ref, ha_ref, ys_ref, u_ref, o_ref, ga_ref, gb_ref, gc_ref, d_ref, wglu_ref, bglu_ref,
                  wa_ref, wb_ref, wc_ref, wo_ref, out_ref, slab_ref):
    ys = ys_ref[...].astype(F32) + d_ref[...] * u_ref[...].astype(F32)
    hs = 0.5 * ys * (1.0 + jnp.tanh(math.sqrt(2.0 / math.pi) * (ys + 0.044715 * (ys * ys * ys))))
    gl = jnp.dot(hs.astype(BF16), wglu_ref[...], preferred_element_type=F32) + bglu_ref[...]
    hs = hs * jax.nn.sigmoid(gl)
    oa = jnp.dot(ha_ref[...], wa_ref[...], preferred_element_type=F32)
    ob = jnp.dot(hs.astype(BF16), wb_ref[...], preferred_element_type=F32)
    oc = jnp.dot(o_ref[...], wc_ref[...], preferred_element_type=F32)
    m = (jax.nn.sigmoid(ga_ref[...].astype(F32)) * oa + jax.nn.sigmoid(gb_ref[...].astype(F32)) * ob
         + jax.nn.sigmoid(gc_ref[...].astype(F32)) * oc)
    out = x_ref[...] + jnp.dot(m.astype(BF16), wo_ref[...], preferred_element_type=F32)
    out_ref[...] = out
    rows = out.shape[0]
    S = slab_ref.shape[0] // rows
    for j in range(S):
        slab_ref[pl.ds(j, rows, stride=S), :] = out[:, j * LANES:(j + 1) * LANES]


def _merge(x, ha, ys, z, ucol, gcol, o, d, wglu, bglu, wa, wb, wc, wo):
    T, D = x.shape
    C, S, AW = ha.shape[1], ys.shape[1], o.shape[1]
    assert ucol % S == 0 and gcol % D == 0
    tm = _tile(T, 256)
    ub, gb = ucol // S, gcol // D
    single = pl.Buffered(1)
    wspec = lambda a: pl.BlockSpec(a.shape, lambda i: (0, 0), pipeline_mode=single)
    return pl.pallas_call(
        _merge_kernel,
        grid=(T // tm,),
        in_specs=[
            pl.BlockSpec((tm, D), lambda i: (i, 0)),
            pl.BlockSpec((tm, C), lambda i: (i, 0)),
            pl.BlockSpec((tm, S), lambda i: (i, 0)),
            pl.BlockSpec((tm, S), lambda i: (i, ub)),
            pl.BlockSpec((tm, AW), lambda i: (i, 0)),
            pl.BlockSpec((tm, D), lambda i: (i, gb)),
            pl.BlockSpec((tm, D), lambda i: (i, gb + 1)),
            pl.BlockSpec((tm, D), lambda i: (i, gb + 2)),
            _const_spec((1, S)), wspec(wglu), _const_spec((1, S)),
            wspec(wa), wspec(wb), wspec(wc), wspec(wo),
        ],
        out_specs=[pl.BlockSpec((tm, D), lambda i: (i, 0)), pl.BlockSpec((tm * (D // LANES), LANES), lambda i: (i, 0))],
        out_shape=[jax.ShapeDtypeStruct((T, D), F32), jax.ShapeDtypeStruct((T * (D // LANES), LANES), F32)],
        compiler_params=_cparams("parallel"),
        name="merge",
    )(x, ha, ys, z, o, z, z, z, d[None], wglu, bglu[None], wa, wb, wc, wo)


def _route_kernel(x_ref, g_ref, wh_ref, wl_ref, b_ref, r_ref, *, ng, epg):
    x = x_ref[...]
    xn = x * lax.rsqrt(jnp.mean(x * x, axis=-1, keepdims=True) + EPS) * g_ref[...]
    xh = xn.astype(BF16)
    xl = (xn - xh.astype(F32)).astype(BF16)
    wh = wh_ref[...]
    lg = (jnp.dot(xh, wh, preferred_element_type=F32) + jnp.dot(xl, wh, preferred_element_type=F32)
          + jnp.dot(xh, wl_ref[...], preferred_element_type=F32) + b_ref[...])
    lane = lax.broadcasted_iota(jnp.int32, (1, LANES), 1).astype(F32)
    big = float(LANES)
    is_g = lane < ng
    gmax = jnp.max(jnp.where(is_g, lg, NEG), axis=-1, keepdims=True)
    den = jnp.sum(jnp.where(is_g, jnp.exp(jnp.where(is_g, lg, NEG) - gmax), 0.0), axis=-1, keepdims=True)
    p_top = 1.0 / den
    g_top = jnp.min(jnp.where(is_g & (lg == gmax), lane, big), axis=-1, keepdims=True)
    sel = (lane >= ng + g_top * epg) & (lane < ng + (g_top + 1.0) * epg)
    le = jnp.where(sel, lg, NEG)
    v1 = jnp.max(le, axis=-1, keepdims=True)
    i1 = jnp.min(jnp.where(sel & (le == v1), lane, big), axis=-1, keepdims=True)
    sel2 = sel & (lane != i1)
    le2 = jnp.where(sel2, lg, NEG)
    v2 = jnp.max(le2, axis=-1, keepdims=True)
    i2 = jnp.min(jnp.where(sel2 & (le2 == v2), lane, big), axis=-1, keepdims=True)
    t = jnp.exp(v2 - v1)
    w1 = p_top / (1.0 + t)
    w2 = p_top * t / (1.0 + t)
    r_ref[...] = jnp.where(lane == 0, i1 - ng, jnp.where(lane == 1, i2 - ng, jnp.where(lane == 2, w1, jnp.where(lane == 3, w2, 0.0))))


def _route(x, g, w_route_g, b_route_g, w_route_e, b_route_e):
    T, D = x.shape
    ng, ne = w_route_g.shape[1], w_route_e.shape[1]
    assert ng + ne <= LANES
    w = jnp.zeros((D, LANES), F32).at[:, :ng].set(w_route_g).at[:, ng:ng + ne].set(w_route_e)
    b = jnp.zeros((1, LANES), F32).at[0, :ng].set(b_route_g).at[0, ng:ng + ne].set(b_route_e)
    wh = w.astype(BF16)
    wl = (w - wh.astype(F32)).astype(BF16)
    tm = _tile(T, 512)
    return pl.pallas_call(
        functools.partial(_route_kernel, ng=ng, epg=ne // ng),
        grid=(T // tm,),
        in_specs=[pl.BlockSpec((tm, D), lambda i: (i, 0)), _const_spec((1, D)),
                  _const_spec((D, LANES)), _const_spec((D, LANES)), _const_spec((1, LANES))],
        out_specs=pl.BlockSpec((tm, LANES), lambda i: (i, 0)),
        out_shape=jax.ShapeDtypeStruct((T, LANES), F32),
        compiler_params=_cparams("parallel"),
        name="route",
    )(x, g[None], wh, wl, b)


def _dispatch_plan(route, n_exp, tmo):
    T = route.shape[0]
    eid = route[:, 0:2].astype(jnp.int32).reshape(-1)
    onehot = (eid[:, None] == jnp.arange(n_exp)[None, :]).astype(jnp.int32)
    csum = jnp.cumsum(onehot, axis=0)
    rank = jnp.sum((csum - onehot) * onehot, axis=1)
    counts = csum[-1]
    padded = ((counts + tmo - 1) // tmo) * tmo
    ends = jnp.cumsum(padded)
    starts = ends - padded
    pos = starts[eid] + rank
    NP = 2 * T + n_exp * tmo
    tok = jnp.zeros((NP,), jnp.int32).at[pos].set(jnp.arange(2 * T, dtype=jnp.int32) // 2)
    tile_start = jnp.arange(NP // tmo, dtype=jnp.int32) * tmo
    tile_exp = jnp.minimum(jnp.sum((tile_start[:, None] >= ends[None, :]).astype(jnp.int32), axis=1), n_exp - 1)
    n_used = (ends[-1] // tmo).astype(jnp.int32).reshape(1)
    return tok, tile_exp, n_used, pos.astype(jnp.int32)


def _pack_bf16_halves(y):
    half = y.shape[1] // 2
    bits = lambda a: lax.bitcast_convert_type(a.astype(BF16).astype(F32), jnp.uint32)
    return (bits(y[:, :half]) >> 16) | (bits(y[:, half:]) & jnp.uint32(0xFFFF0000))


def _unpack_bf16_halves(w):
    lo = lax.bitcast_convert_type(w << 16, F32)
    hi = lax.bitcast_convert_type(w & jnp.uint32(0xFFFF0000), F32)
    return lo, hi


def _moe_kernel(te_ref, tok_ref, nu_ref, x_hbm, g_ref, wg_ref, wu_ref, wd_ref, y_ref, xbuf, sem, *, tmo):
    i = pl.program_id(0)
    n_used = nu_ref[0]
    slot = i % 2

    S = xbuf.shape[1] // tmo

    def row_copy(tile, dst_slot, r):
        t = pl.multiple_of(tok_ref[tile * tmo + r] * S, S)
        return pltpu.make_async_copy(x_hbm.at[pl.ds(t, S), :], xbuf.at[dst_slot, pl.ds(r * S, S), :], sem.at[dst_slot])

    def experts(gather_next):
        if gather_next:
            for r in range(tmo):
                row_copy(i + 1, 1 - slot, r).start()
        pltpu.make_async_copy(x_hbm.at[pl.ds(0, tmo * S), :], xbuf.at[slot], sem.at[slot]).wait()
        x = jnp.concatenate([xbuf[slot, pl.ds(j, tmo, stride=S), :] for j in range(S)], axis=1)
        xn = (x * lax.rsqrt(jnp.mean(x * x, axis=-1, keepdims=True) + EPS) * g_ref[...]).astype(BF16)
        hg = jnp.dot(xn, wg_ref[0], preferred_element_type=F32)
        hu = jnp.dot(xn, wu_ref[0], preferred_element_type=F32)
        h = (hg * jax.nn.sigmoid(hg) * hu).astype(BF16)
        y = _pack_bf16_halves(jnp.dot(h, wd_ref[0], preferred_element_type=F32))
        sy = y_ref.shape[0] // tmo
        for j in range(sy):
            y_ref[pl.ds(j, tmo, stride=sy), :] = y[:, j * LANES:(j + 1) * LANES]

    @pl.when(i == 0)
    def _():
        def issue(r, c):
            row_copy(0, 0, r).start()
            return c
        lax.fori_loop(0, tmo, issue, 0)

    @pl.when(i + 1 < n_used)
    def _():
        experts(True)

    @pl.when(i + 1 == n_used)
    def _():
        experts(False)

    @pl.when(i >= n_used)
    def _():
        y_ref[...] = jnp.zeros_like(y_ref)


def _moe(x_slab, g, plan, wg, wu, wd, tmo):
    E, D, DE = wg.shape
    S, SY = D // LANES, D // 2 // LANES
    tok, tile_exp, n_used, _ = plan
    NP = tok.shape[0]
    grid_spec = pltpu.PrefetchScalarGridSpec(
        num_scalar_prefetch=3,
        grid=(NP // tmo,),
        in_specs=[
            pl.BlockSpec(memory_space=pl.ANY),
            pl.BlockSpec((1, D), lambda i, te, tk, nu: (0, 0)),
            pl.BlockSpec((1, D, DE), lambda i, te, tk, nu: (te[i], 0, 0)),
            pl.BlockSpec((1, D, DE), lambda i, te, tk, nu: (te[i], 0, 0)),
            pl.BlockSpec((1, DE, D), lambda i, te, tk, nu: (te[i], 0, 0)),
        ],
        out_specs=pl.BlockSpec((tmo * SY, LANES), lambda i, te, tk, nu: (i, 0)),
        scratch_shapes=[pltpu.VMEM((2, tmo * S, LANES), F32), pltpu.SemaphoreType.DMA((2,))],
    )
    return pl.pallas_call(
        functools.partial(_moe_kernel, tmo=tmo),
        grid_spec=grid_spec,
        out_shape=jax.ShapeDtypeStruct((NP * SY, LANES), jnp.uint32),
        compiler_params=_cparams("arbitrary"),
        name="moe_experts",
    )(tile_exp, tok, n_used, x_slab, g[None], wg, wu, wd)


def _combine_kernel(pos_ref, x_ref, r_ref, y_hbm, out_ref, ybuf, sem, *, tc):
    i = pl.program_id(0)
    n = pl.num_programs(0)
    slot = i % 2

    SY = ybuf.shape[2] // tc

    def row_copy(tile, dst_slot, r, s):
        p = pl.multiple_of(pos_ref[2 * (tile * tc + r) + s] * SY, SY)
        return pltpu.make_async_copy(y_hbm.at[pl.ds(p, SY), :], ybuf.at[dst_slot, s, pl.ds(r * SY, SY), :],
                                     sem.at[dst_slot])

    @pl.when(i == 0)
    def _():
        def issue(r, c):
            for s in range(2):
                row_copy(0, 0, r, s).start()
            return c
        lax.fori_loop(0, tc, issue, 0)

    @pl.when(i + 1 < n)
    def _():
        for r in range(tc):
            for s in range(2):
                row_copy(i + 1, 1 - slot, r, s).start(priority=s)

    for s in range(2):
        pltpu.make_async_copy(y_hbm.at[pl.ds(0, tc * SY), :], ybuf.at[slot, s], sem.at[slot]).wait()
    half = out_ref.shape[1] // 2
    rows = lambda s: jnp.concatenate([ybuf[slot, s, pl.ds(j, tc, stride=SY), :] for j in range(SY)], axis=1)
    lo0, hi0 = _unpack_bf16_halves(rows(0))
    lo1, hi1 = _unpack_bf16_halves(rows(1))
    w0 = r_ref[:, 2:3]
    w1 = r_ref[:, 3:4]
    out_ref[:, :half] = x_ref[:, :half] + (w0 * lo0 + w1 * lo1)
    out_ref[:, half:] = x_ref[:, half:] + (w0 * hi0 + w1 * hi1)


def _combine(x, route, y_sorted, pos):
    T, D = x.shape
    tc = _tile(T, 256)
    grid_spec = pltpu.PrefetchScalarGridSpec(
        num_scalar_prefetch=1,
        grid=(T // tc,),
        in_specs=[pl.BlockSpec((tc, D), lambda i, p: (i, 0)), pl.BlockSpec((tc, LANES), lambda i, p: (i, 0)),
                  pl.BlockSpec(memory_space=pl.ANY)],
        out_specs=pl.BlockSpec((tc, D), lambda i, p: (i, 0)),
        scratch_shapes=[pltpu.VMEM((2, 2, tc * (D // 2 // LANES), LANES), jnp.uint32), pltpu.SemaphoreType.DMA((2,))],
    )
    return pl.pallas_call(
        functools.partial(_combine_kernel, tc=tc),
        grid_spec=grid_spec,
        out_shape=jax.ShapeDtypeStruct((T, D), F32),
        compiler_params=_cparams("arbitrary"),
        name="moe_combine",
    )(pos, x, route, y_sorted)


def _layer_weights(p, H):
    D = p['w_in'].shape[0]
    C = p['w_dw'].shape[1]
    S = p['ssm_d'].shape[0]
    AW = H * p['q_gain'].shape[0]
    o_u = 2 * C
    o_q = o_u + S
    o_f = o_q + 3 * AW
    o_g = o_f + H
    perm = lambda a: jnp.concatenate([a[..., o_g:], a[..., o_q:o_f], a[..., :o_u], a[..., o_u:o_q]], axis=-1)
    cols = dict(g=0, q=3 * D, conv=3 * D + 3 * AW, u=3 * D + 3 * AW + 2 * C)
    wf = jnp.zeros((D, LANES), F32).at[:, :H].set(p['w_in'][:, o_f:o_g]).astype(BF16)
    bf = jnp.zeros((1, LANES), F32).at[0, :H].set(p['b_in'][o_f:o_g])
    return perm(p['w_in']).astype(BF16), perm(p['b_in'])[None], wf, bf, cols


def _mixer_and_moe(x3, p, conv_buf, ssm_h0, past, H, layer, depth, kv_prev):
    B, L, D = x3.shape
    T = B * L
    x = x3.reshape(T, D)
    hd = p['q_gain'].shape[0]
    AW = H * hd
    K, C = p['w_dw'].shape
    G, P, gw = p['ssm_b_re'].shape
    S = G * gw
    w_main, b_main, wf, bf, cols = _layer_weights(p, H)
    z, logf = _in_proj(x, p['g_mix'][None], w_main, b_main, wf, bf)

    buf0 = jnp.zeros((B, CONV_HALO, C), F32)
    if conv_buf is not None:
        buf0 = buf0.at[:, CONV_HALO - (K - 1):].set(conv_buf)
    ha, nc = _conv_branch(z, cols['conv'], B, L, buf0, p['w_dw'], p['b_dw'], p['ln_g'], p['ln_b'])
    new_conv = nc[:, CONV_HALO - (K - 1):]

    mats = _ssm_matrices(p['ssm_a_re'], p['ssm_a_im'], p['ssm_log_dt'], p['ssm_b_re'], p['ssm_b_im'],
                         p['ssm_c_re'], p['ssm_c_im'])
    h0 = jnp.zeros((B, G, P, 2), F32) if ssm_h0 is None else ssm_h0
    if (L // SSM_CHUNK) % 8 == 0:
        ys, new_ssm = _ssm_branch_natural(z, cols['u'], B, L, mats, h0)
    else:
        ys, new_ssm = _ssm_branch(z[:, cols['u']:cols['u'] + S], B, L, mats, h0)

    vcol = cols['q'] + 2 * AW
    qa, ka, k5, v5 = _qk_prep(z, cols['q'], B, L, logf, p['q_gain'], p['k_gain'], H, layer, depth, kv_prev)
    if past is None:
        tq = _tile(L, 256, LANES)
        o = _attention(qa, ka, z, vcol, B, L, H, hd, tq, 2 * tq if L % (2 * tq) == 0 else tq)
    else:
        nk = ((L + LANES - 1) // LANES) * LANES
        pad_rows = lambda a: jnp.pad(a.reshape(B, L, -1), ((0, 0), (0, nk - L), (0, 0))).reshape(B * nk, -1)
        o = _cached_attention(qa, pad_rows(ka), pad_rows(z[:, vcol:vcol + AW]), *past, layer, B, L, H, hd)

    bfw = lambda a: a.astype(BF16)
    x, x_slab = _merge(x, ha, ys, z, cols['u'], cols['g'], o, p['ssm_d'], bfw(p['w_glu']), p['b_glu'],
               bfw(p['w_conv_out']), bfw(p['w_ssm_out']), bfw(p['w_attn_out']), bfw(p['w_out']))

    route = _route(x, p['g_ffn'], p['w_route_g'], p['b_route_g'], p['w_route_e'], p['b_route_e'])
    E = p['w_e_gate'].shape[0]
    tmo = _tile(T, 256 if 2 * T >= 256 * E else 64)
    plan = _dispatch_plan(route, E, tmo)
    y_sorted = _moe(x_slab, p['g_ffn'], plan, bfw(p['w_e_gate']), bfw(p['w_e_up']), bfw(p['w_e_down']), tmo)
    x = _combine(x, route, y_sorted, plan[3])

    state = (new_conv, new_ssm, logf[:, :H].reshape(B, L, H))
    return x.reshape(B, L, D), state, (k5, v5)


def kernel(x_prompt, x_sample, cache_conv, state_ssm, cache_k, cache_v, cache_logf, g_mix, w_in, b_in, w_dw, b_dw, ln_g, ln_b, w_conv_out, ssm_a_re, ssm_a_im, ssm_log_dt, ssm_b_re, ssm_b_im, ssm_c_re, ssm_c_im, ssm_d, w_glu, b_glu, w_ssm_out, q_gain, k_gain, w_attn_out, w_out, g_ffn, w_route_g, b_route_g, w_route_e, b_route_e, w_e_gate, w_e_up, w_e_down):
    depth = w_in.shape[0]
    H = cache_logf.shape[-1]
    stacked = dict(g_mix=g_mix, w_in=w_in, b_in=b_in, w_dw=w_dw, b_dw=b_dw, ln_g=ln_g, ln_b=ln_b,
                   w_conv_out=w_conv_out, ssm_a_re=ssm_a_re, ssm_a_im=ssm_a_im, ssm_log_dt=ssm_log_dt,
                   ssm_b_re=ssm_b_re, ssm_b_im=ssm_b_im, ssm_c_re=ssm_c_re, ssm_c_im=ssm_c_im, ssm_d=ssm_d,
                   w_glu=w_glu, b_glu=b_glu, w_ssm_out=w_ssm_out, q_gain=q_gain, k_gain=k_gain,
                   w_attn_out=w_attn_out, w_out=w_out, g_ffn=g_ffn, w_route_g=w_route_g, b_route_g=b_route_g,
                   w_route_e=w_route_e, b_route_e=b_route_e, w_e_gate=w_e_gate, w_e_up=w_e_up, w_e_down=w_e_down)
    xp, xs = x_prompt, x_sample
    st_p, st_s = [], []
    kv_p = kv_s = None
    nb, PL, hd = cache_k.shape[1], cache_k.shape[2], cache_k.shape[4]
    past = (jnp.transpose(cache_k, (0, 1, 3, 4, 2)).reshape(depth, nb, H * hd, PL),
            jnp.transpose(cache_v, (0, 1, 3, 4, 2)).reshape(depth, nb, H * hd, PL),
            jnp.transpose(cache_logf, (0, 1, 3, 2)))
    for l in range(depth):
        p = {k: v[l] for k, v in stacked.items()}
        xp, st, kv_p = _mixer_and_moe(xp, p, None, None, None, H, l, depth, kv_p)
        st_p.append(st)
        xs, st, kv_s = _mixer_and_moe(xs, p, cache_conv[l], state_ssm[l], past, H, l, depth, kv_s)
        st_s.append(st)
    stack = lambda sts, i: jnp.stack([s[i] for s in sts])

    def kv_out(a):
        return a if a.ndim == 5 else jnp.transpose(a.reshape(a.shape[0], a.shape[1], H, hd, a.shape[3]), (0, 1, 4, 2, 3))

    return ((xp, xs) + (stack(st_p, 0), stack(st_p, 1), kv_out(kv_p[0]), kv_out(kv_p[1]), stack(st_p, 2))
            + (stack(st_s, 0), stack(st_s, 1), kv_out(kv_s[0]), kv_out(kv_s[1]), stack(st_s, 2)))
```
